```python
import jax, jax.numpy as jnp
from jax import lax
import numpy as np

D_MODEL = 1024
BATCH = 4
SEQ = 4096
DEPTH = 4
DEC_BATCH = 32
DEC_SEQ = 8
PAST_LEN = 8192
PAGE_SIZE = 128

D_MIX = D_MODEL
D_ATT = D_MIX // 2
D_CONV = D_MIX - D_ATT
N_HEADS = 8
HEAD_DIM = D_ATT // N_HEADS
N_KV = 2
GROUP = N_HEADS // N_KV
D_KV = N_KV * HEAD_DIM
CMP_BLOCK = 32
SEL_BLOCK = 64
CMP_PER_SEL = SEL_BLOCK // CMP_BLOCK
TOP_N = 16
WINDOW = 512
CONV_W = 3
D_PLE = 256
Q_BLOCK = 128
N_KV_SLOTS = 4
EPS = 1e-6
NEG = -1e30
INVALID = -1e9
FORCE_BONUS = 1e3
IN_SPLITS = (D_ATT, D_KV, D_KV, D_KV, D_KV, D_KV, D_KV, 3 * N_HEADS, D_ATT,
             D_CONV, D_CONV, D_CONV, D_CONV)
D_IN = 2 * D_ATT + 6 * D_KV + 3 * N_HEADS + 4 * D_CONV

kernel_name = 'hymba_nsa_shortconv_ple_step'


def rmsnorm(x, g):
    xf = x.astype(jnp.float32)
    y = xf * lax.rsqrt(jnp.mean(xf * xf, -1, keepdims=True) + EPS)
    return (y * g.astype(jnp.float32)).astype(x.dtype)


def alibi_slopes():
    h = jnp.arange(1, N_HEADS + 1, dtype=jnp.float32)
    return jnp.exp2(-8.0 * h / N_HEADS).reshape(N_KV, GROUP)


def masked_softmax(s, mask):
    s = jnp.where(mask, s, NEG)
    e = jnp.where(mask, jnp.exp(s - jnp.max(s, -1, keepdims=True)), 0.0)
    return e / jnp.maximum(jnp.sum(e, -1, keepdims=True), 1e-30)


def split_cols(a):
    outs, o = [], 0
    for n in IN_SPLITS:
        outs.append(a[..., o:o + n])
        o += n
    return outs


def in_proj(x, g_norm, w_in, g_q, g_k):
    bn, L = x.shape[:2]
    h = rmsnorm(x, g_norm)
    q, kc, vc, ks, vs, kw, vw, gl, za, bg, cg, hv, zb = split_cols(h @ w_in)
    heads = lambda a, n: a.reshape(bn, L, n, HEAD_DIM)
    q = rmsnorm(heads(q, N_HEADS), g_q).reshape(bn, L, N_KV, GROUP, HEAD_DIM)
    ks = rmsnorm(heads(ks, N_KV), g_k[1])
    kw = rmsnorm(heads(kw, N_KV), g_k[2])
    gates = jax.nn.sigmoid(gl.reshape(bn, L, N_KV, GROUP, 3))
    u = cg * hv
    return (q, heads(kc, N_KV), heads(vc, N_KV), ks, heads(vs, N_KV), kw, heads(vw, N_KV),
            gates, za, bg, u, zb)


def compress(rows, pos, w):
    bn, T = rows.shape[:2]
    blocks = rows.reshape(bn, T // CMP_BLOCK, CMP_BLOCK, N_KV, HEAD_DIM) + pos[:, None, :]
    return jnp.einsum('bclkd,lde->bcke', blocks, w)


def global_keys(kc, vc, ks, vs, cmp_pos, w_phi, g_kc):
    bn, T = kc.shape[:2]
    t_pad = -(-T // SEL_BLOCK) * SEL_BLOCK
    pad = lambda a: jnp.pad(a, ((0, 0), (0, t_pad - T), (0, 0), (0, 0)))
    kcmp = rmsnorm(compress(pad(kc), cmp_pos[0], w_phi[0]), g_kc)
    vcmp = compress(pad(vc), cmp_pos[1], w_phi[1])
    nb = t_pad // SEL_BLOCK
    blk = lambda a: pad(a).reshape(bn, nb, SEL_BLOCK, N_KV, HEAD_DIM).transpose(0, 3, 1, 2, 4)
    return kcmp, vcmp, blk(ks), blk(vs)


def nsa_core(q, t_pos, gates, kcmp, vcmp, ksb, vsb, kw, vw, k_pos):
    f32 = jnp.float32
    bn, nq = q.shape[:2]
    m = alibi_slopes()
    qf = q.astype(f32) * HEAD_DIM ** -0.5
    tp = t_pos[:, None]
    nc = kcmp.shape[1]
    c_end = jnp.arange(nc) * CMP_BLOCK + (CMP_BLOCK - 1)
    dist = tp - c_end[None, :]
    s = (jnp.einsum('bqkgd,bckd->bqkgc', qf, kcmp.astype(f32))
         - m[:, :, None] * dist[:, None, None, :].astype(f32))
    p_cmp = masked_softmax(s, (dist >= 0)[:, None, None, :])
    o_cmp = jnp.einsum('bqkgc,bckd->bqkgd', p_cmp, vcmp.astype(f32))
    nb = ksb.shape[2]
    imp = p_cmp.sum(3).reshape(bn, nq, N_KV, nb, CMP_PER_SEL).sum(-1)
    blk = jnp.arange(nb)[None, :]
    cur = tp // SEL_BLOCK
    valid = blk * SEL_BLOCK <= tp
    forced = (blk == 0) | (blk == cur) | (blk == cur - 1)
    score = jnp.where(valid[None, :, None, :],
                      imp + jnp.where(forced, FORCE_BONUS, 0.0)[None, :, None, :], INVALID)
    _, idx = lax.top_k(score, min(TOP_N, nb))
    n = idx.shape[-1]
    bi = jnp.arange(bn)[:, None, None, None]
    hi = jnp.arange(N_KV)[None, None, :, None]
    kb = ksb[bi, hi, idx].astype(f32)
    vb = vsb[bi, hi, idx].astype(f32)
    key_pos = idx[..., None] * SEL_BLOCK + jnp.arange(SEL_BLOCK)
    dist = t_pos[None, :, None, None, None] - key_pos
    s = (jnp.einsum('bqkgd,bqknsd->bqkgns', qf, kb)
         - m[None, None, :, :, None, None] * dist[:, :, :, None].astype(f32))
    s = s.reshape(bn, nq, N_KV, GROUP, n * SEL_BLOCK)
    mask = (dist >= 0).reshape(bn, nq, N_KV, 1, n * SEL_BLOCK)
    p = masked_softmax(s, mask).reshape(bn, nq, N_KV, GROUP, n, SEL_BLOCK)
    o_sel = jnp.einsum('bqkgns,bqknsd->bqkgd', p, vb)
    dist = tp - k_pos[None, :]
    mask = ((dist >= 0) & (dist < WINDOW) & (k_pos[None, :] >= 0))[:, None, None, :]
    s = (jnp.einsum('bqkgd,blkd->bqkgl', qf, kw.astype(f32))
         - m[:, :, None] * dist[:, None, None, :].astype(f32))
    o_win = jnp.einsum('bqkgl,blkd->bqkgd', masked_softmax(s, mask), vw.astype(f32))
    g = gates.astype(f32)
    return g[..., 0:1] * o_cmp + g[..., 1:2] * o_sel + g[..., 2:3] * o_win


def short_conv(u, prev, w):
    L = u.shape[1]
    up = jnp.concatenate([prev.astype(u.dtype), u], 1)
    y = sum(w[k] * up[:, k:k + L] for k in range(CONV_W))
    return y, up[:, up.shape[1] - (CONV_W - 1):]


def mix_out(o_att, za, bg, y_conv, zb, w_out):
    o = jnp.concatenate([o_att * jax.nn.silu(za), bg * y_conv * jax.nn.silu(zb)], -1)
    return o @ w_out


def ple_add(x, p, w_ple, w_pg, g_ple):
    return x + (p @ w_ple) * jax.nn.sigmoid(rmsnorm(x, g_ple) @ w_pg)


def prompt_layer(x, p, g_norm, w_in, g_q, g_k, cmp_pos, w_phi, conv_w, w_out, w_ple, w_pg, g_ple):
    bn, S = x.shape[:2]
    q, kc, vc, ks, vs, kw, vw, gates, za, bg, u, zb = in_proj(x, g_norm, w_in, g_q, g_k)
    kcmp, vcmp, ksb, vsb = global_keys(kc, vc, ks, vs, cmp_pos, w_phi, g_k[0])
    kvw = jnp.stack([kw, vw], 2)
    kvw_pad = jnp.pad(kvw, ((0, 0), (WINDOW, 0), (0, 0), (0, 0), (0, 0)))

    def qblock(qi):
        q0 = qi * Q_BLOCK
        sl = lambda a: lax.dynamic_slice_in_dim(a, q0, Q_BLOCK, 1)
        band = lax.dynamic_slice_in_dim(kvw_pad, q0, WINDOW + Q_BLOCK, 1)
        t_pos = q0 + jnp.arange(Q_BLOCK)
        k_pos = q0 - WINDOW + jnp.arange(WINDOW + Q_BLOCK)
        return nsa_core(sl(q), t_pos, sl(gates), kcmp, vcmp, ksb, vsb,
                        band[:, :, 0], band[:, :, 1], k_pos)

    o = lax.map(qblock, jnp.arange(S // Q_BLOCK))
    o = jnp.moveaxis(o, 0, 1).reshape(bn, S, D_ATT).astype(x.dtype)
    yc, conv_state = short_conv(u, jnp.zeros((bn, CONV_W - 1, D_CONV), u.dtype), conv_w)
    x = x + mix_out(o, za, bg, yc, zb, w_out)
    x = ple_add(x, p, w_ple, w_pg, g_ple)
    new_kv = jnp.stack([kc, vc, ks, vs], 2)
    new_win = kvw[:, S - min(WINDOW, S):]
    return x, new_kv, new_win, conv_state


def sample_layer(x, p, cache_kv_l, cache_win_l, state_conv_l, page_table,
                 g_norm, w_in, g_q, g_k, cmp_pos, w_phi, conv_w, w_out, w_ple, w_pg, g_ple):
    bn, L = x.shape[:2]
    past = cache_kv_l[page_table]
    past = past.reshape(bn, -1, N_KV_SLOTS, N_KV, HEAD_DIM)
    P = past.shape[1]
    q, kc, vc, ks, vs, kw, vw, gates, za, bg, u, zb = in_proj(x, g_norm, w_in, g_q, g_k)
    cat = lambda a, j: jnp.concatenate([past[:, :, j].astype(a.dtype), a], 1)
    kcmp, vcmp, ksb, vsb = global_keys(cat(kc, 0), cat(vc, 1), cat(ks, 2), cat(vs, 3),
                                       cmp_pos, w_phi, g_k[0])
    wb = cache_win_l.shape[1]
    kvw = jnp.concatenate([cache_win_l.astype(x.dtype), jnp.stack([kw, vw], 2)], 1)
    t_pos = P + jnp.arange(L)
    k_pos = P - wb + jnp.arange(wb + L)
    o = nsa_core(q, t_pos, gates, kcmp, vcmp, ksb, vsb, kvw[:, :, 0], kvw[:, :, 1], k_pos)
    o = o.reshape(bn, L, D_ATT).astype(x.dtype)
    yc, conv_state = short_conv(u, state_conv_l, conv_w)
    x = x + mix_out(o, za, bg, yc, zb, w_out)
    x = ple_add(x, p, w_ple, w_pg, g_ple)
    new_kv = jnp.stack([kc, vc, ks, vs], 2)
    return x, new_kv, kvw[:, L:], conv_state


def setup_inputs(seed: int = 0) -> dict:
    key = jax.random.key(seed)
    ks = jax.random.split(key, 20)
    n_pages = PAST_LEN // PAGE_SIZE
    n_used = DEC_BATCH * n_pages
    n_pool = n_used + max(1, n_used // 4)
    page_table = jax.random.permutation(ks[0], n_pool)[:n_used].astype(jnp.int32)
    page_table = page_table.reshape(DEC_BATCH, n_pages)
    wb = min(WINDOW, PAST_LEN)
    nrm = lambda k, shape, s=1.0: s * jax.random.normal(k, shape, jnp.float32)
    return {
        'x_prompt': nrm(ks[1], (BATCH, SEQ, D_MODEL)),
        'x_sample': nrm(ks[2], (DEC_BATCH, DEC_SEQ, D_MODEL)),
        'cache_kv': nrm(ks[3], (DEPTH, n_pool, PAGE_SIZE, N_KV_SLOTS, N_KV, HEAD_DIM)),
        'cache_win': nrm(ks[4], (DEPTH, DEC_BATCH, wb, 2, N_KV, HEAD_DIM)),
        'state_conv': nrm(ks[5], (DEPTH, DEC_BATCH, CONV_W - 1, D_CONV)),
        'page_table': page_table,
        'p_prompt': nrm(ks[6], (DEPTH, BATCH, SEQ, D_PLE)),
        'p_sample': nrm(ks[7], (DEPTH, DEC_BATCH, DEC_SEQ, D_PLE)),
        'g_norm': 1.0 + nrm(ks[8], (DEPTH, D_MODEL), 0.02),
        'w_in': nrm(ks[9], (DEPTH, D_MODEL, D_IN), D_MODEL ** -0.5),
        'g_q': 1.0 + nrm(ks[10], (DEPTH, HEAD_DIM), 0.02),
        'g_k': 1.0 + nrm(ks[11], (DEPTH, 3, HEAD_DIM), 0.02),
        'cmp_pos': nrm(ks[12], (DEPTH, 2, CMP_BLOCK, HEAD_DIM), 0.02),
        'w_phi': nrm(ks[13], (DEPTH, 2, CMP_BLOCK, HEAD_DIM, HEAD_DIM), (CMP_BLOCK * HEAD_DIM) ** -0.5),
        'conv_w': nrm(ks[14], (DEPTH, CONV_W, D_CONV), CONV_W ** -0.5),
        'w_out': nrm(ks[15], (DEPTH, D_MIX, D_MODEL), D_MIX ** -0.5),
        'w_ple': nrm(ks[16], (DEPTH, D_PLE, D_MODEL), D_PLE ** -0.5),
        'w_pg': nrm(ks[17], (DEPTH, D_MODEL, D_MODEL), D_MODEL ** -0.5),
        'g_ple': 1.0 + nrm(ks[18], (DEPTH, D_MODEL), 0.02),
    }


def reference(x_prompt, x_sample, cache_kv, cache_win, state_conv, page_table, p_prompt, p_sample,
              g_norm, w_in, g_q, g_k, cmp_pos, w_phi, conv_w, w_out, w_ple, w_pg, g_ple):
    xp, xs = x_prompt, x_sample
    kv_p, win_p, conv_p, kv_s, win_s, conv_s = [], [], [], [], [], []
    for i in range(DEPTH):
        w = (g_norm[i], w_in[i], g_q[i], g_k[i], cmp_pos[i], w_phi[i], conv_w[i], w_out[i],
             w_ple[i], w_pg[i], g_ple[i])
        xp, a, b, c = prompt_layer(xp, p_prompt[i], *w)
        xs, d, e, f = sample_layer(xs, p_sample[i], cache_kv[i], cache_win[i], state_conv[i],
                                   page_table, *w)
        kv_p.append(a); win_p.append(b); conv_p.append(c)
        kv_s.append(d); win_s.append(e); conv_s.append(f)
    return (xp, xs, jnp.stack(kv_p), jnp.stack(win_p), jnp.stack(conv_p),
            jnp.stack(kv_s), jnp.stack(win_s), jnp.stack(conv_s))
```

```python
import functools

import numpy as np
import jax
import jax.numpy as jnp
from jax import lax
from jax.experimental import pallas as pl
from jax.experimental.pallas import tpu as pltpu

D_MODEL = 1024
N_HEADS = 8
HEAD_DIM = 64
N_KV = 2
GROUP = 4
D_ATT = 512
D_CONV = 512
D_KV = 128
CMP_BLOCK = 32
SEL_BLOCK = 64
TOP_N = 16
WINDOW = 512
CONV_W = 3
D_PLE = 256
PAGE_SIZE = 128
EPS = 1e-6
NEG = -1e30
INVALID = -1e9
FORCE_BONUS = 1e3
BIG = 1e30

LANES = 128
Q_TILE = 128
SEL_CHUNK = 512
VMEM_LIMIT = 56 * 1024 * 1024

F32 = jnp.float32
BF16 = jnp.bfloat16

C_Q, C_KV, C_WIN, C_ZA, C_BG, C_CG, C_HV, C_ZB, C_GL, C_END = (
    0, 512, 1024, 1280, 1792, 2304, 2816, 3328, 3840, 3968)

_NT = (((1,), (1,)), ((), ()))


def _slope(h):
    return 2.0 ** (-(h + 1))


def _sigmoid(x):
    return 1.0 / (1.0 + jnp.exp(-x))


def _lane_iota(shape):
    return lax.broadcasted_iota(jnp.int32, shape, len(shape) - 1)


def _row_iota(shape):
    return lax.broadcasted_iota(jnp.int32, shape, 0)


def _group_norm(a, g, gmat):
    n = a.shape[1]
    msq = jnp.dot((a * a).astype(BF16), gmat[:n, :n], preferred_element_type=F32)
    return a * lax.rsqrt(msq + EPS) * g


def _place_pair(a_even, a_odd, k):
    lane = _lane_iota(a_even.shape)
    e_src = a_even if k == 0 else pltpu.roll(a_even, 64, axis=1)
    o_src = pltpu.roll(a_odd, 64, axis=1) if k == 0 else a_odd
    return jnp.where(lane < 64, e_src, o_src)


def _inproj_kernel(seq_len, tm, *refs):
    carried = seq_len >= tm
    if carried:
        (x_ref, gn_ref, w_ref, gq_ref, gks_ref, gkw_ref, gm_ref, cw_ref,
         q_out, kv_out, win_out, gate_out, sa_out, cp_out, u_out, carry) = refs
    else:
        (x_ref, gn_ref, w_ref, gq_ref, gks_ref, gkw_ref, gm_ref, cw_ref, i1_ref, i2_ref,
         q_out, kv_out, win_out, gate_out, sa_out, cp_out, u_out) = refs

    x = x_ref[0]
    ms = jnp.mean(x * x, axis=-1, keepdims=True)
    h = (x * lax.rsqrt(ms + EPS) * gn_ref[0]).astype(BF16)
    gmat = gm_ref[...]

    def mm(lo, hi):
        return jnp.dot(h, w_ref[0, :, lo:hi], preferred_element_type=F32)

    aq = mm(C_Q, C_KV)
    gq = gq_ref[0]
    for j in range(2):
        sl = slice(j * 256, (j + 1) * 256)
        qn = _group_norm(aq[:, sl], gq[:, sl], gmat) * (HEAD_DIM ** -0.5)
        q_out[0, :, sl] = qn.astype(BF16)

    akv = mm(C_KV, C_WIN)
    kv_out[0, :, 0:256] = akv[:, 0:256]
    kv_out[0, :, 256:384] = _group_norm(akv[:, 256:384], gks_ref[0], gmat)
    kv_out[0, :, 384:512] = akv[:, 384:512]

    aw = mm(C_WIN, C_ZA)
    win_out[0, :, 0:128] = _group_norm(aw[:, 0:128], gkw_ref[0], gmat)
    win_out[0, :, 128:256] = aw[:, 128:256]

    za = mm(C_ZA, C_BG)
    sa_out[0] = za * _sigmoid(za)

    gate_out[0] = _sigmoid(mm(C_GL, C_END))

    u = mm(C_CG, C_HV) * mm(C_HV, C_ZB)
    row = _row_iota(u.shape)
    r1 = pltpu.roll(u, 1, axis=0)
    r2 = pltpu.roll(u, 2, axis=0)
    if carried:
        @pl.when(pl.program_id(1) == 0)
        def _():
            carry[...] = jnp.zeros_like(carry)
        prev = carry[...]
        um1 = jnp.where(row == 0, prev[7:8], r1)
        um2 = jnp.where(row == 0, prev[6:7], jnp.where(row == 1, prev[7:8], r2))
        carry[...] = u[tm - 8:]
        u_out[0] = u[tm - 8:]
    else:
        pos = row % seq_len
        um1 = jnp.where(pos >= 1, r1, i1_ref[0])
        um2 = jnp.where(pos >= 2, r2, i2_ref[0])
        u_out[0] = u
    cw = cw_ref[0]
    y = cw[0:1] * um2 + cw[1:2] * um1 + cw[2:3] * u
    zb = mm(C_ZB, C_GL)
    cp_out[0] = (mm(C_BG, C_CG) * y * (zb * _sigmoid(zb))).astype(BF16)


def _inproj(layer, x, w_r, gn, gq_t, gks_t, gkw_t, gmat, conv_w, seq_len, tm, inits=None):
    nb, t = x.shape[:2]
    carried = seq_len >= tm
    grid = (nb, t // tm)
    tok = lambda n: pl.BlockSpec((1, tm, n), lambda b, s: (b, s, 0))
    lay2 = lambda n: pl.BlockSpec((1, 1, n), lambda b, s: (layer, 0, 0))
    in_specs = [
        tok(D_MODEL),
        lay2(D_MODEL),
        pl.BlockSpec((1, D_MODEL, C_END), lambda b, s: (layer, 0, 0)),
        lay2(512), lay2(128), lay2(128),
        pl.BlockSpec((256, 256), lambda b, s: (0, 0)),
        pl.BlockSpec((1, CONV_W, D_CONV), lambda b, s: (layer, 0, 0)),
    ]
    args = [x, gn, w_r, gq_t, gks_t, gkw_t, gmat, conv_w]
    scratch = []
    if carried:
        u_spec = pl.BlockSpec((1, 8, D_CONV), lambda b, s: (b, 0, 0))
        u_shape = (nb, 8, D_CONV)
        scratch = [pltpu.VMEM((8, D_CONV), F32)]
    else:
        in_specs += [tok(D_CONV), tok(D_CONV)]
        args += list(inits)
        u_spec = tok(D_CONV)
        u_shape = (nb, t, D_CONV)
    out_shape = [
        jax.ShapeDtypeStruct((nb, t, 512), BF16),
        jax.ShapeDtypeStruct((nb, t, 512), F32),
        jax.ShapeDtypeStruct((nb, t, 256), F32),
        jax.ShapeDtypeStruct((nb, t, 128), F32),
        jax.ShapeDtypeStruct((nb, t, 512), F32),
        jax.ShapeDtypeStruct((nb, t, 512), BF16),
        jax.ShapeDtypeStruct(u_shape, F32),
    ]
    out_specs = [tok(512), tok(512), tok(256), tok(128), tok(512), tok(512), u_spec]
    return pl.pallas_call(
        functools.partial(_inproj_kernel, seq_len, tm),
        grid=grid, in_specs=in_specs, out_specs=out_specs, out_shape=out_shape,
        scratch_shapes=scratch,
        compiler_params=pltpu.CompilerParams(
            dimension_semantics=("arbitrary", "arbitrary"), vmem_limit_bytes=VMEM_LIMIT),
        name="in_proj",
    )(*args)


def _compress_rows(load, n_half, wc_ref, pos_ref, gk_ref, gm_ref, out_ref):
    acc = jnp.zeros((2 * n_half, 256), F32)
    for l in range(CMP_BLOCK):
        xl = jnp.concatenate(
            [jnp.concatenate([load(half, l, n_half, 2 * CMP_BLOCK),
                              load(half, CMP_BLOCK + l, n_half, 2 * CMP_BLOCK)], axis=0)
             for half in range(2)], axis=1)
        xl = (xl + pos_ref[0, l:l + 1, :]).astype(BF16)
        acc = acc + jnp.dot(xl, wc_ref[0, l], preferred_element_type=F32)
    out_ref[0, :, 0:128] = _group_norm(acc[:, 0:128], gk_ref[0], gm_ref[...])
    out_ref[0, :, 128:256] = acc[:, 128:256]


def _compress_prompt_kernel(n_half, kc_ref, vc_ref, wc_ref, pos_ref, gk_ref, gm_ref, out_ref):
    refs = (kc_ref, vc_ref)
    load = lambda half, start, n, stride: refs[half][0, pl.ds(start, n, stride=stride), :]
    _compress_rows(load, n_half, wc_ref, pos_ref, gk_ref, gm_ref, out_ref)


def _compress_prompt(layer, kv, wc, posc, gk0_t, gmat):
    nb, s = kv.shape[:2]
    nc = s // CMP_BLOCK
    return pl.pallas_call(
        functools.partial(_compress_prompt_kernel, nc // 2),
        grid=(nb,),
        in_specs=[
            pl.BlockSpec((1, s, 128), lambda b: (b, 0, 0)),
            pl.BlockSpec((1, s, 128), lambda b: (b, 0, 1)),
            pl.BlockSpec((1, CMP_BLOCK, 256, 256), lambda b: (layer, 0, 0, 0)),
            pl.BlockSpec((1, CMP_BLOCK, 256), lambda b: (layer, 0, 0)),
            pl.BlockSpec((1, 1, 128), lambda b: (layer, 0, 0)),
            pl.BlockSpec((256, 256), lambda b: (0, 0)),
        ],
        out_specs=pl.BlockSpec((1, nc, 256), lambda b: (b, 0, 0)),
        out_shape=jax.ShapeDtypeStruct((nb, nc, 256), F32),
        compiler_params=pltpu.CompilerParams(
            dimension_semantics=("arbitrary",), vmem_limit_bytes=VMEM_LIMIT),
        name="compress_prompt",
    )(kv, kv, wc, posc, gk0_t, gmat)


def _topk_rows(score, blk, n):
    sel = jnp.zeros(score.shape, F32)
    for _ in range(n):
        mx = jnp.max(score, axis=0, keepdims=True)
        idx = jnp.min(jnp.where(score == mx, blk, 1 << 20), axis=0, keepdims=True)
        hit = blk == idx
        sel = jnp.where(hit, 1.0, sel)
        score = jnp.where(hit, -3e38, score)
    return sel


def _topk_lanes(score, blk, n):
    sel = jnp.zeros(score.shape, F32)
    for _ in range(n):
        mx = jnp.max(score, axis=1, keepdims=True)
        idx = jnp.min(jnp.where(score == mx, blk, 1 << 20), axis=1, keepdims=True)
        hit = blk == idx
        sel = jnp.where(hit, 1.0, sel)
        score = jnp.where(hit, -3e38, score)
    return sel


def _expand_gates(g, e_ref):
    g_hi = g.astype(BF16)
    g_lo = (g - g_hi.astype(F32)).astype(BF16)
    e = e_ref[...]
    return (jnp.dot(g_hi, e, preferred_element_type=F32)
            + jnp.dot(g_lo, e, preferred_element_type=F32))


def _attn_prompt_kernel(s_len, q_ref, kv_ref, win_ref, cmp_ref, gate_ref, sa_ref,
                        pc_ref, mk_ref, cc_ref, e_ref, o_ref,
                        ksa, vsb, kwa, vwb, kca, vcb, qa, m_scr, l_scr, acc_scr):
    qi = pl.program_id(1)
    q0 = qi * Q_TILE
    n_cmp = s_len // CMP_BLOCK
    n_blk = s_len // SEL_BLOCK

    @pl.when(qi == 0)
    def _build():
        def bld(c, carry):
            r = pl.ds(pl.multiple_of(c * 512, 512), 512)
            kvb = kv_ref[0, r, :]
            wb = win_ref[0, r, :]
            pc = pc_ref[r, :]
            lane = _lane_iota(pc.shape)
            for k in range(N_KV):
                ks = kvb[:, 0:128] if k == 0 else pltpu.roll(kvb[:, 0:128], 64, axis=1)
                kw = wb[:, 0:128] if k == 0 else pltpu.roll(wb[:, 0:128], 64, axis=1)
                ksa[k, r, 0:128] = jnp.where(lane < 64, ks, pc).astype(BF16)
                ksa[k, r, 128:256] = mk_ref[r, :]
                kwa[k, r, :] = jnp.where(lane < 64, kw, pc).astype(BF16)
            vsb[r, :] = kvb[:, 128:256].astype(BF16)
            vwb[r, :] = wb[:, 128:256].astype(BF16)
            return carry
        lax.fori_loop(0, s_len // 512, bld, 0)
        cm = cmp_ref[0]
        cc = cc_ref[...]
        lane = _lane_iota(cc.shape)
        for k in range(N_KV):
            kc = cm[:, 0:128] if k == 0 else pltpu.roll(cm[:, 0:128], 64, axis=1)
            kca[k] = jnp.where(lane < 64, kc, cc).astype(BF16)
        vcb[...] = cm[:, 128:256].astype(BF16)

    qf = q_ref[0].astype(F32)
    lane128 = _lane_iota((Q_TILE, LANES))
    trow = q0 + (_row_iota((GROUP * Q_TILE, 1)) % Q_TILE)

    o_tiles = [[None] * 4 for _ in range(3)]

    for k in range(N_KV):
        for h in range(GROUP):
            hg = k * GROUP + h
            tile = qf[:, (hg // 2) * 128:(hg // 2 + 1) * 128]
            src = tile if hg % 2 == 0 else pltpu.roll(tile, 64, axis=1)
            m = _slope(hg)
            bias = jnp.where(lane128 == 64, 64.0 * m,
                             jnp.where(lane128 == 65, m,
                                       jnp.where(lane128 == 66, -m * q0.astype(F32), 0.0)))
            qa[k, h * Q_TILE:(h + 1) * Q_TILE, 0:128] = jnp.where(lane128 < 64, src, bias).astype(BF16)

        st = lax.dot_general(kca[k], qa[k, :, 0:128], _NT, preferred_element_type=F32)
        r = _row_iota(st.shape)
        cend = CMP_BLOCK * (2 * (r % (n_cmp // 2)) + r // (n_cmp // 2)) + (CMP_BLOCK - 1)
        tq = q0 + (_lane_iota(st.shape) % Q_TILE)
        valid = cend <= tq
        st = jnp.where(valid, st, NEG)
        mx = jnp.max(st, axis=0, keepdims=True)
        e = jnp.where(valid, jnp.exp(st - mx), 0.0)
        pt = e / jnp.maximum(jnp.sum(e, axis=0, keepdims=True), 1e-30)

        imp = pt[:, 0:128] + pt[:, 128:256] + pt[:, 256:384] + pt[:, 384:512]
        imp = imp[0:n_blk] + imp[n_blk:2 * n_blk]
        blk = _row_iota(imp.shape)
        tcol = q0 + _lane_iota(imp.shape)
        cur = tcol // SEL_BLOCK
        forced = (blk == 0) | (blk == cur) | (blk == cur - 1)
        score = jnp.where(blk * SEL_BLOCK <= tcol,
                          imp + jnp.where(forced, FORCE_BONUS, 0.0), INVALID)
        sel = _topk_rows(score, blk, min(TOP_N, n_blk))
        notsel = jnp.concatenate([1.0 - sel, jnp.zeros((LANES - n_blk, Q_TILE), F32)], axis=0)
        notsel = notsel.T.astype(BF16)
        for h in range(GROUP):
            qa[k, h * Q_TILE:(h + 1) * Q_TILE, 128:256] = notsel

        oc = []
        for h in range(GROUP):
            p_h = pt[:, h * Q_TILE:(h + 1) * Q_TILE].T.astype(BF16)
            oc.append(jnp.dot(p_h, vcb[...], preferred_element_type=F32))

        m_scr[...] = jnp.full(m_scr.shape, -3e38, F32)
        l_scr[...] = jnp.zeros(l_scr.shape, F32)
        acc_scr[...] = jnp.zeros(acc_scr.shape, F32)

        def sel_body(c, carry):
            off = pl.multiple_of(c * SEL_CHUNK, SEL_CHUNK)
            kblk = ksa[k, pl.ds(off, SEL_CHUNK), :]
            s = lax.dot_general(qa[k], kblk, _NT, preferred_element_type=F32)
            kpos = off + _lane_iota(s.shape)
            s = jnp.where(kpos <= trow, s, NEG)
            m_old = m_scr[...]
            m_new = jnp.maximum(m_old, jnp.max(s, axis=1, keepdims=True))
            alpha = jnp.exp(m_old - m_new)
            p = jnp.exp(s - m_new)
            l_scr[...] = alpha * l_scr[...] + jnp.sum(p, axis=1, keepdims=True)
            acc_scr[...] = alpha * acc_scr[...] + jnp.dot(
                p.astype(BF16), vsb[pl.ds(off, SEL_CHUNK), :], preferred_element_type=F32)
            m_scr[...] = m_new
            return carry
        lax.fori_loop(0, (q0 + Q_TILE + SEL_CHUNK - 1) // SEL_CHUNK, sel_body, 0)
        osel = acc_scr[...] / l_scr[...]

        w_keys = WINDOW + Q_TILE
        start = pl.multiple_of(jnp.maximum(q0 - WINDOW, 0), Q_TILE)
        sw = lax.dot_general(qa[k, :, 0:128], kwa[k, pl.ds(start, w_keys), :], _NT,
                             preferred_element_type=F32)
        dist = trow - (start + _lane_iota(sw.shape))
        sw = jnp.where(dist.astype(jnp.uint32) < WINDOW, sw, NEG)
        mw = jnp.max(sw, axis=1, keepdims=True)
        pw = jnp.exp(sw - mw)
        lw = jnp.sum(pw, axis=1, keepdims=True)
        ow = jnp.dot(pw.astype(BF16), vwb[pl.ds(start, w_keys), :],
                     preferred_element_type=F32) / lw

        for pair in range(GROUP // 2):
            h0, h1 = 2 * pair, 2 * pair + 1
            rows = lambda a, h: a[h * Q_TILE:(h + 1) * Q_TILE]
            o_tiles[0][k * 2 + pair] = _place_pair(oc[h0], oc[h1], k)
            o_tiles[1][k * 2 + pair] = _place_pair(rows(osel, h0), rows(osel, h1), k)
            o_tiles[2][k * 2 + pair] = _place_pair(rows(ow, h0), rows(ow, h1), k)

    ge = _expand_gates(gate_ref[0], e_ref)
    sa = sa_ref[0]
    for t in range(4):
        sl = slice(t * 128, (t + 1) * 128)
        o = (ge[:, sl] * o_tiles[0][t] + ge[:, 512 + t * 128:512 + (t + 1) * 128] * o_tiles[1][t]
             + ge[:, 1024 + t * 128:1024 + (t + 1) * 128] * o_tiles[2][t])
        o_ref[0, :, sl] = (o * sa[:, sl]).astype(BF16)


def _attn_prompt(q, kv, win, cmpk, gates, sa, pc, mk, cc, emat):
    nb, s = q.shape[:2]
    nq = s // Q_TILE
    n_cmp = s // CMP_BLOCK
    tokq = lambda n: pl.BlockSpec((1, Q_TILE, n), lambda b, i: (b, i, 0))
    const = lambda a: pl.BlockSpec(a.shape, lambda b, i: (0,) * a.ndim)
    return pl.pallas_call(
        functools.partial(_attn_prompt_kernel, s),
        grid=(nb, nq),
        in_specs=[
            tokq(512),
            pl.BlockSpec((1, s, 256), lambda b, i: (b, 0, 1)),
            pl.BlockSpec((1, s, 256), lambda b, i: (b, 0, 0)),
            pl.BlockSpec((1, n_cmp, 256), lambda b, i: (b, 0, 0)),
            tokq(128), tokq(512),
            const(pc), const(mk), const(cc), const(emat),
        ],
        out_specs=tokq(512),
        out_shape=jax.ShapeDtypeStruct((nb, s, 512), BF16),
        scratch_shapes=[
            pltpu.VMEM((N_KV, s, 256), BF16),
            pltpu.VMEM((s, 128), BF16),
            pltpu.VMEM((N_KV, s, 128), BF16),
            pltpu.VMEM((s, 128), BF16),
            pltpu.VMEM((N_KV, n_cmp, 128), BF16),
            pltpu.VMEM((n_cmp, 128), BF16),
            pltpu.VMEM((N_KV, GROUP * Q_TILE, 256), BF16),
            pltpu.VMEM((GROUP * Q_TILE, 1), F32),
            pltpu.VMEM((GROUP * Q_TILE, 1), F32),
            pltpu.VMEM((GROUP * Q_TILE, 128), F32),
        ],
        compiler_params=pltpu.CompilerParams(
            dimension_semantics=("arbitrary", "arbitrary"), vmem_limit_bytes=VMEM_LIMIT),
        name="attn_prompt",
    )(q, kv, win, cmpk, gates, sa, pc, mk, cc, emat)


def _outproj_kernel(x_ref, oa_ref, cp_ref, p_ref, wo_ref, wple_ref, wpg_ref, g_ref, y_ref):
    x1 = (x_ref[0]
          + jnp.dot(oa_ref[0], wo_ref[0, 0:D_ATT, :], preferred_element_type=F32)
          + jnp.dot(cp_ref[0], wo_ref[0, D_ATT:, :], preferred_element_type=F32))
    ms = jnp.mean(x1 * x1, axis=-1, keepdims=True)
    h2 = (x1 * lax.rsqrt(ms + EPS) * g_ref[0]).astype(BF16)
    gate = _sigmoid(jnp.dot(h2, wpg_ref[0], preferred_element_type=F32))
    pe = jnp.dot(p_ref[0, 0].astype(BF16), wple_ref[0], preferred_element_type=F32)
    y_ref[0] = x1 + pe * gate


def _outproj(layer, x, oa, cp, p_all, wo, wple, wpg, g_ple, tm):
    nb, t = x.shape[:2]
    tok = lambda n: pl.BlockSpec((1, tm, n), lambda b, s: (b, s, 0))
    return pl.pallas_call(
        _outproj_kernel,
        grid=(nb, t // tm),
        in_specs=[
            tok(D_MODEL), tok(512), tok(512),
            pl.BlockSpec((1, 1, tm, D_PLE), lambda b, s: (layer, b, s, 0)),
            pl.BlockSpec((1, D_MODEL, D_MODEL), lambda b, s: (layer, 0, 0)),
            pl.BlockSpec((1, D_PLE, D_MODEL), lambda b, s: (layer, 0, 0)),
            pl.BlockSpec((1, D_MODEL, D_MODEL), lambda b, s: (layer, 0, 0)),
            pl.BlockSpec((1, 1, D_MODEL), lambda b, s: (layer, 0, 0)),
        ],
        out_specs=tok(D_MODEL),
        out_shape=jax.ShapeDtypeStruct((nb, t, D_MODEL), F32),
        compiler_params=pltpu.CompilerParams(
            dimension_semantics=("arbitrary", "arbitrary"), vmem_limit_bytes=VMEM_LIMIT),
        name="out_proj",
    )(x, oa, cp, p_all, wo, wple, wpg, g_ple)


def _page_copy(cache_ref, pt_ref, buf, sem, b, slot, n_pages, row0):
    return [pltpu.make_async_copy(
        cache_ref.at[pt_ref[b, p], pl.ds(row0, 256), :],
        buf.at[slot, pl.ds(p * 256, 256), :],
        sem.at[slot]) for p in range(n_pages)]


def _paged_prefetch(cache_ref, pt_ref, buf, sem, n_pages, row0):
    b = pl.program_id(0)
    slot = b % 2

    @pl.when(b == 0)
    def _():
        for cp in _page_copy(cache_ref, pt_ref, buf, sem, 0, 0, n_pages, row0):
            cp.start()

    @pl.when(b + 1 < pl.num_programs(0))
    def _():
        for cp in _page_copy(cache_ref, pt_ref, buf, sem, b + 1, 1 - slot, n_pages, row0):
            cp.start()

    for cp in _page_copy(cache_ref, pt_ref, buf, sem, b, slot, n_pages, row0):
        cp.wait()
    return slot


def _compress_sample_kernel(n_pages, pt_ref, cache_ref, wc_ref, pos_ref, gk_ref, gm_ref,
                            out_ref, buf, sem, xk, xv):
    slot = _paged_prefetch(cache_ref, pt_ref, buf, sem, n_pages, 0)

    def to_rows(p, carry):
        xt = buf[slot, pl.ds(pl.multiple_of(p * 256, 256), 256), :].T
        r = pl.ds(pl.multiple_of(p * PAGE_SIZE, PAGE_SIZE), PAGE_SIZE)
        xk[r, :] = xt[:, 0:128]
        xv[r, :] = xt[:, 128:256]
        return carry
    lax.fori_loop(0, n_pages, to_rows, 0)

    n_half = n_pages * PAGE_SIZE // CMP_BLOCK // 2
    refs = (xk, xv)
    load = lambda half, start, n, stride: refs[half][pl.ds(start, n, stride=stride), :]
    _compress_rows(load, n_half, wc_ref, pos_ref, gk_ref, gm_ref, out_ref)


def _compress_sample(layer, pt, cache3, wc, posc, gk0_t, gmat):
    nb, n_pages = pt.shape
    nc = n_pages * PAGE_SIZE // CMP_BLOCK
    grid_spec = pltpu.PrefetchScalarGridSpec(
        num_scalar_prefetch=1,
        grid=(nb,),
        in_specs=[
            pl.BlockSpec(memory_space=pl.ANY),
            pl.BlockSpec((1, CMP_BLOCK, 256, 256), lambda b, pt: (layer, 0, 0, 0)),
            pl.BlockSpec((1, CMP_BLOCK, 256), lambda b, pt: (layer, 0, 0)),
            pl.BlockSpec((1, 1, 128), lambda b, pt: (layer, 0, 0)),
            pl.BlockSpec((256, 256), lambda b, pt: (0, 0)),
        ],
        out_specs=pl.BlockSpec((1, nc, 256), lambda b, pt: (b, 0, 0)),
        scratch_shapes=[
            pltpu.VMEM((2, n_pages * 256, PAGE_SIZE), F32),
            pltpu.SemaphoreType.DMA((2,)),
            pltpu.VMEM((n_pages * PAGE_SIZE, 128), F32),
            pltpu.VMEM((n_pages * PAGE_SIZE, 128), F32),
        ],
    )
    return pl.pallas_call(
        functools.partial(_compress_sample_kernel, n_pages),
        grid_spec=grid_spec,
        out_shape=jax.ShapeDtypeStruct((nb, nc, 256), F32),
        compiler_params=pltpu.CompilerParams(
            dimension_semantics=("arbitrary",), vmem_limit_bytes=VMEM_LIMIT),
        name="compress_sample",
    )(pt, cache3, wc, posc, gk0_t, gmat)


def _attn_sample_kernel(n_pages, dec_len, pt_ref, cache_ref, q_ref, kvn_ref, wn_ref,
                        cwin_ref, cmp_ref, gate_ref, sa_ref, mk_ref, e_ref, o_ref,
                        buf, sem, kta, vtb):
    past = n_pages * PAGE_SIZE
    n_cmp = past // CMP_BLOCK
    n_blk = past // SEL_BLOCK
    rows = N_KV * GROUP * dec_len
    hrows = GROUP * dec_len

    @pl.when(pl.program_id(0) == 0)
    def _():
        kta[128:256, :] = mk_ref[...]

    slot = _paged_prefetch(cache_ref, pt_ref, buf, sem, n_pages, 256)
    for p in range(n_pages):
        blk = buf[slot, p * 256:(p + 1) * 256, :]
        kta[0:128, p * PAGE_SIZE:(p + 1) * PAGE_SIZE] = blk[0:128].astype(BF16)
        vtb[:, p * PAGE_SIZE:(p + 1) * PAGE_SIZE] = blk[128:256].astype(BF16)

    qf = q_ref[0].astype(F32)
    lane = _lane_iota((dec_len, LANES))
    pieces = []
    for k in range(N_KV):
        for h in range(GROUP):
            hg = k * GROUP + h
            tile = qf[:, (hg // 2) * 128:(hg // 2 + 1) * 128]
            src = tile if (hg % 2) == k else pltpu.roll(tile, 64, axis=1)
            keep = (lane < 64) if k == 0 else (lane >= 64)
            pieces.append(jnp.where(keep, src, 0.0))
    qbd = jnp.concatenate(pieces, axis=0)
    qbd16 = qbd.astype(BF16)

    rid = _row_iota((rows, 1))
    slope = jnp.zeros((rows, 1), F32)
    for hg in range(N_HEADS):
        slope = jnp.where(rid // dec_len == hg, _slope(hg), slope)
    qpos = rid % dec_len

    cm = cmp_ref[0]
    sc = lax.dot_general(qbd16, cm[:, 0:128].astype(BF16), _NT, preferred_element_type=F32)
    j = _lane_iota(sc.shape)
    cend = CMP_BLOCK * (2 * (j % (n_cmp // 2)) + j // (n_cmp // 2)) + (CMP_BLOCK - 1)
    sc = sc + slope * (cend - past).astype(F32)
    mc = jnp.max(sc, axis=1, keepdims=True)
    ec = jnp.exp(sc - mc)
    pcm = ec / jnp.maximum(jnp.sum(ec, axis=1, keepdims=True), 1e-30)
    ocm = jnp.dot(pcm.astype(BF16), cm[:, 128:256].astype(BF16), preferred_element_type=F32)

    ns_pieces = []
    for k in range(N_KV):
        imp = pcm[k * hrows:k * hrows + dec_len]
        for h in range(1, GROUP):
            imp = imp + pcm[k * hrows + h * dec_len:k * hrows + (h + 1) * dec_len]
        imp = imp[:, 0:n_blk] + imp[:, n_blk:2 * n_blk]
        blk = _lane_iota(imp.shape)
        t = past + _row_iota(imp.shape)
        cur = t // SEL_BLOCK
        forced = (blk == 0) | (blk == cur) | (blk == cur - 1)
        score = imp + jnp.where(forced, FORCE_BONUS, 0.0)
        sel = _topk_lanes(score, blk, TOP_N - 1)
        ns = 1.0 - sel
        ns_pieces += [ns] * GROUP
    notsel = jnp.concatenate(ns_pieces, axis=0)
    if n_blk < LANES:
        notsel = jnp.concatenate([notsel, jnp.zeros((rows, LANES - n_blk), F32)], axis=1)
    qaug = jnp.concatenate([qbd16, notsel.astype(BF16)], axis=1)

    s1 = jnp.dot(qaug, kta[...], preferred_element_type=F32)
    s1 = s1 + slope * (_lane_iota(s1.shape) - past).astype(F32)
    kn = jnp.concatenate([kvn_ref[0][:, 256:384], jnp.zeros((LANES - dec_len, 128), F32)], axis=0)
    vn = jnp.concatenate([kvn_ref[0][:, 384:512], jnp.zeros((LANES - dec_len, 128), F32)], axis=0)
    s2 = lax.dot_general(qbd16, kn.astype(BF16), _NT, preferred_element_type=F32)
    j2 = _lane_iota(s2.shape)
    s2 = jnp.where(j2 <= qpos, s2 + slope * j2.astype(F32), NEG)
    ms = jnp.maximum(jnp.max(s1, axis=1, keepdims=True), jnp.max(s2, axis=1, keepdims=True))
    p1 = jnp.exp(s1 - ms)
    p2 = jnp.exp(s2 - ms)
    ls = jnp.sum(p1, axis=1, keepdims=True) + jnp.sum(p2, axis=1, keepdims=True)
    osl = (lax.dot_general(p1.astype(BF16), vtb[...], _NT, preferred_element_type=F32)
           + jnp.dot(p2.astype(BF16), vn.astype(BF16), preferred_element_type=F32)) / ls

    cw = cwin_ref[0, 0]
    wb = cw.shape[1]
    w1 = jnp.dot(qbd16, cw[0:128].astype(BF16), preferred_element_type=F32)
    jw = _lane_iota(w1.shape)
    dist = qpos + (wb - jw)
    w1 = jnp.where(dist < WINDOW, w1 + slope * (jw - wb).astype(F32), NEG)
    kwn = jnp.concatenate([wn_ref[0][:, 0:128], jnp.zeros((LANES - dec_len, 128), F32)], axis=0)
    vwn = jnp.concatenate([wn_ref[0][:, 128:256], jnp.zeros((LANES - dec_len, 128), F32)], axis=0)
    w2 = lax.dot_general(qbd16, kwn.astype(BF16), _NT, preferred_element_type=F32)
    w2 = jnp.where(j2 <= qpos, w2 + slope * j2.astype(F32), NEG)
    mw = jnp.maximum(jnp.max(w1, axis=1, keepdims=True), jnp.max(w2, axis=1, keepdims=True))
    pw1 = jnp.exp(w1 - mw)
    pw2 = jnp.exp(w2 - mw)
    lw = jnp.sum(pw1, axis=1, keepdims=True) + jnp.sum(pw2, axis=1, keepdims=True)
    owd = (lax.dot_general(pw1.astype(BF16), cw[128:256].astype(BF16), _NT,
                           preferred_element_type=F32)
           + jnp.dot(pw2.astype(BF16), vwn.astype(BF16), preferred_element_type=F32)) / lw

    ge = _expand_gates(gate_ref[0], e_ref)
    sa = sa_ref[0]
    for t4 in range(4):
        k, pair = t4 // 2, t4 % 2
        r0 = (k * GROUP + 2 * pair) * dec_len
        r1 = r0 + dec_len
        rows_of = lambda a, r: a[r:r + dec_len]
        sl = slice(t4 * 128, (t4 + 1) * 128)
        o = (ge[:, sl] * _place_pair(rows_of(ocm, r0), rows_of(ocm, r1), k)
             + ge[:, 512 + t4 * 128:512 + (t4 + 1) * 128]
             * _place_pair(rows_of(osl, r0), rows_of(osl, r1), k)
             + ge[:, 1024 + t4 * 128:1024 + (t4 + 1) * 128]
             * _place_pair(rows_of(owd, r0), rows_of(owd, r1), k))
        o_ref[0, :, sl] = (o * sa[:, sl]).astype(BF16)


def _attn_sample(layer, pt, cache3, q, kvn, wn, cwin, cmpk, gates, sa, mk, emat):
    nb, n_pages = pt.shape
    dec_len = q.shape[1]
    past = n_pages * PAGE_SIZE
    wb = cwin.shape[3]
    tok = lambda n: pl.BlockSpec((1, dec_len, n), lambda b, pt: (b, 0, 0))
    grid_spec = pltpu.PrefetchScalarGridSpec(
        num_scalar_prefetch=1,
        grid=(nb,),
        in_specs=[
            pl.BlockSpec(memory_space=pl.ANY),
            tok(512), tok(512), tok(256),
            pl.BlockSpec((1, 1, 256, wb), lambda b, pt: (layer, b, 0, 0)),
            pl.BlockSpec((1, past // CMP_BLOCK, 256), lambda b, pt: (b, 0, 0)),
            tok(128), tok(512),
            pl.BlockSpec(mk.shape, lambda b, pt: (0, 0)),
            pl.BlockSpec(emat.shape, lambda b, pt: (0, 0)),
        ],
        out_specs=tok(512),
        scratch_shapes=[
            pltpu.VMEM((2, n_pages * 256, PAGE_SIZE), F32),
            pltpu.SemaphoreType.DMA((2,)),
            pltpu.VMEM((256, past), BF16),
            pltpu.VMEM((128, past), BF16),
        ],
    )
    return pl.pallas_call(
        functools.partial(_attn_sample_kernel, n_pages, dec_len),
        grid_spec=grid_spec,
        out_shape=jax.ShapeDtypeStruct((nb, dec_len, 512), BF16),
        compiler_params=pltpu.CompilerParams(
            dimension_semantics=("arbitrary",), vmem_limit_bytes=VMEM_LIMIT),
        name="attn_sample",
    )(pt, cache3, q, kvn, wn, cwin, cmpk, gates, sa, mk, emat)


def _key_consts(n_keys, n_cmp):
    kpos = np.arange(n_keys)
    pc = np.zeros((n_keys, LANES), np.float32)
    pc[:, 64] = kpos // 64
    pc[:, 65] = kpos % 64
    pc[:, 66] = 1.0
    mk = np.zeros((n_keys, LANES), np.float32)
    mk[kpos, (kpos // SEL_BLOCK) % LANES] = -BIG
    r = np.arange(n_cmp)
    cend = CMP_BLOCK * (2 * (r % (n_cmp // 2)) + r // (n_cmp // 2)) + CMP_BLOCK - 1
    cc = np.zeros((n_cmp, LANES), np.float32)
    cc[:, 64] = cend // 64
    cc[:, 65] = cend % 64
    cc[:, 66] = 1.0
    return jnp.asarray(pc), jnp.asarray(mk, dtype=BF16), jnp.asarray(cc)


def _block_mask_rows(n_keys):
    kpos = np.arange(n_keys)
    mk = np.zeros((LANES, n_keys), np.float32)
    mk[(kpos // SEL_BLOCK) % LANES, kpos] = -BIG
    return jnp.asarray(mk, dtype=BF16)


def _gate_expand_matrix():
    e = np.zeros((LANES, 3 * D_ATT), np.float32)
    for h in range(N_HEADS):
        for j in range(3):
            e[h * 3 + j, j * D_ATT + h * HEAD_DIM:j * D_ATT + (h + 1) * HEAD_DIM] = 1.0
    return jnp.asarray(e, dtype=BF16)


def _group_mean_matrix():
    i = np.arange(256)
    return jnp.asarray((i[:, None] // HEAD_DIM == i[None, :] // HEAD_DIM) / HEAD_DIM, dtype=BF16)


def _reorder_w_in(w_in):
    o = np.cumsum([0, 512, 128, 128, 128, 128, 128, 128, 24, 512, 512, 512, 512, 512])
    q, kc, vc, ks, vs, kw, vw, gl, za, bg, cg, hv, zb = [
        w_in[..., o[i]:o[i + 1]] for i in range(13)]
    pad = jnp.zeros(w_in.shape[:-1] + (LANES - 24,), w_in.dtype)
    return jnp.concatenate([q, kc, vc, ks, vs, kw, vw, za, bg, cg, hv, zb, gl, pad],
                           axis=-1).astype(BF16)


def _compress_weights(w_phi, cmp_pos):
    depth = w_phi.shape[0]
    wc = jnp.zeros((depth, CMP_BLOCK, 256, 256), F32)
    for slot in range(2):
        for k in range(N_KV):
            o = (slot * N_KV + k) * HEAD_DIM
            wc = wc.at[:, :, o:o + HEAD_DIM, o:o + HEAD_DIM].set(w_phi[:, slot])
    posc = jnp.concatenate([cmp_pos[:, 0], cmp_pos[:, 0], cmp_pos[:, 1], cmp_pos[:, 1]], axis=-1)
    return wc.astype(BF16), posc


def kernel(x_prompt, x_sample, cache_kv, cache_win, state_conv, page_table, p_prompt, p_sample,
           g_norm, w_in, g_q, g_k, cmp_pos, w_phi, conv_w, w_out, w_ple, w_pg, g_ple):
    depth = w_in.shape[0]
    nb, s_len = x_prompt.shape[:2]
    db, dec_len = x_sample.shape[:2]
    n_pool = cache_kv.shape[1]
    n_pages = page_table.shape[1]
    past = n_pages * PAGE_SIZE
    wb = cache_win.shape[2]

    w_r = _reorder_w_in(w_in)
    wo16, wple16, wpg16 = w_out.astype(BF16), w_ple.astype(BF16), w_pg.astype(BF16)
    wc, posc = _compress_weights(w_phi, cmp_pos)
    gq_t = jnp.tile(g_q, (1, N_HEADS))[:, None]
    gk0_t = jnp.tile(g_k[:, 0], (1, N_KV))[:, None]
    gks_t = jnp.tile(g_k[:, 1], (1, N_KV))[:, None]
    gkw_t = jnp.tile(g_k[:, 2], (1, N_KV))[:, None]
    g_norm = g_norm[:, None]
    g_ple = g_ple[:, None]
    gmat = _group_mean_matrix()
    emat = _gate_expand_matrix()
    pc_p, mk_p, cc_p = _key_consts(s_len, s_len // CMP_BLOCK)
    mk_s = _block_mask_rows(past)

    cache3 = jnp.transpose(cache_kv, (0, 1, 3, 4, 5, 2)).reshape(
        depth * n_pool, 4 * N_KV * HEAD_DIM, PAGE_SIZE)
    cwin4 = jnp.transpose(cache_win, (0, 1, 3, 4, 5, 2)).reshape(
        depth, db, 2 * N_KV * HEAD_DIM, wb)
    xs = x_sample.reshape(1, db * dec_len, D_MODEL)
    ps = p_sample.reshape(depth, 1, db * dec_len, D_PLE)

    xp = x_prompt
    outs = [[] for _ in range(6)]
    for i in range(depth):
        q, kv, win, gates, sa, cp, tail = _inproj(
            i, xp, w_r, g_norm, gq_t, gks_t, gkw_t, gmat, conv_w, s_len, 512)
        cmpk = _compress_prompt(i, kv, wc, posc, gk0_t, gmat)
        oa = _attn_prompt(q, kv, win, cmpk, gates, sa, pc_p, mk_p, cc_p, emat)
        xp = _outproj(i, xp, oa, cp, p_prompt, wo16, wple16, wpg16, g_ple, 512)
        outs[0].append(kv.reshape(nb, s_len, 4, N_KV, HEAD_DIM))
        outs[1].append(win[:, s_len - min(WINDOW, s_len):].reshape(nb, -1, 2, N_KV, HEAD_DIM))
        outs[2].append(tail[:, 8 - (CONV_W - 1):])

        st = state_conv[i]
        pad = jnp.zeros((db, dec_len - 1, D_CONV), F32)
        init1 = jnp.concatenate([st[:, 1:2], pad], axis=1).reshape(1, db * dec_len, D_CONV)
        init2 = jnp.concatenate([st[:, 0:1], st[:, 1:2], pad[:, 1:]], axis=1).reshape(
            1, db * dec_len, D_CONV)
        q, kv, win, gates, sa, cp, u = _inproj(
            i, xs, w_r, g_norm, gq_t, gks_t, gkw_t, gmat, conv_w, dec_len, db * dec_len,
            inits=(init1, init2))
        pt = page_table + i * n_pool
        cmpk = _compress_sample(i, pt, cache3, wc, posc, gk0_t, gmat)
        r3 = lambda a: a.reshape(db, dec_len, a.shape[-1])
        oa = _attn_sample(i, pt, cache3, r3(q), r3(kv), r3(win), cwin4, cmpk, r3(gates), r3(sa),
                          mk_s, emat)
        xs = _outproj(i, xs, oa.reshape(1, db * dec_len, 512), cp, ps, wo16, wple16, wpg16,
                      g_ple, db * dec_len)
        outs[3].append(kv.reshape(db, dec_len, 4, N_KV, HEAD_DIM))
        new_win = jnp.concatenate(
            [cwin4[i][:, :, dec_len:], jnp.transpose(r3(win), (0, 2, 1))], axis=2)
        outs[4].append(jnp.transpose(new_win.reshape(db, 2, N_KV, HEAD_DIM, wb), (0, 4, 1, 2, 3)))
        outs[5].append(u.reshape(db, dec_len, D_CONV)[:, dec_len - (CONV_W - 1):])

    return (xp, xs.reshape(db, dec_len, D_MODEL)) + tuple(jnp.stack(o) for o in outs)
```

```python
import functools

import numpy as np
import jax
import jax.numpy as jnp
from jax import lax
from jax.experimental import pallas as pl
from jax.experimental.pallas import tpu as pltpu

D_MODEL = 1024
N_HEADS = 8
HEAD_DIM = 64
N_KV = 2
GROUP = 4
D_ATT = 512
D_CONV = 512
D_KV = 128
CMP_BLOCK = 32
SEL_BLOCK = 64
TOP_N = 16
WINDOW = 512
CONV_W = 3
D_PLE = 256
PAGE_SIZE = 128
EPS = 1e-6
NEG = -1e30
INVALID = -1e9
FORCE_BONUS = 1e3
BIG = 1e30

LANES = 128
Q_TILE = 128
SEL_CHUNK = 256
VMEM_LIMIT = 56 * 1024 * 1024

F32 = jnp.float32
BF16 = jnp.bfloat16

C_Q, C_KV, C_WIN, C_ZA, C_BG, C_CG, C_HV, C_ZB, C_GL, C_END = (
    0, 512, 1024, 1280, 1792, 2304, 2816, 3328, 3840, 3968)

_NT = (((1,), (1,)), ((), ()))


def _slope(h):
    return 2.0 ** (-(h + 1))


def _sigmoid(x):
    return 1.0 / (1.0 + jnp.exp(-x))


def _lane_iota(shape):
    return lax.broadcasted_iota(jnp.int32, shape, len(shape) - 1)


def _row_iota(shape):
    return lax.broadcasted_iota(jnp.int32, shape, 0)


def _group_norm(a, g, gmat):
    n = a.shape[1]
    msq = jnp.dot((a * a).astype(BF16), gmat[:n, :n], preferred_element_type=F32)
    return a * lax.rsqrt(msq + EPS) * g


def _place_pair(a_even, a_odd, k):
    lane = _lane_iota(a_even.shape)
    e_src = a_even if k == 0 else pltpu.roll(a_even, 64, axis=1)
    o_src = pltpu.roll(a_odd, 64, axis=1) if k == 0 else a_odd
    return jnp.where(lane < 64, e_src, o_src)


def _inproj_kernel(seq_len, tm, *refs):
    carried = seq_len >= tm
    if carried:
        (x_ref, gn_ref, w_ref, gq_ref, gks_ref, gkw_ref, gm_ref, cw_ref,
         q_out, kv_out, win_out, gate_out, sa_out, cp_out, u_out, carry) = refs
    else:
        (x_ref, gn_ref, w_ref, gq_ref, gks_ref, gkw_ref, gm_ref, cw_ref, i1_ref, i2_ref,
         q_out, kv_out, win_out, gate_out, sa_out, cp_out, u_out) = refs

    x = x_ref[0]
    ms = jnp.mean(x * x, axis=-1, keepdims=True)
    h = (x * lax.rsqrt(ms + EPS) * gn_ref[0]).astype(BF16)
    gmat = gm_ref[...]

    def mm(lo, hi):
        return jnp.dot(h, w_ref[0, :, lo:hi], preferred_element_type=F32)

    aq = mm(C_Q, C_KV)
    gq = gq_ref[0]
    for j in range(2):
        sl = slice(j * 256, (j + 1) * 256)
        qn = _group_norm(aq[:, sl], gq[:, sl], gmat) * (HEAD_DIM ** -0.5)
        q_out[0, :, sl] = qn.astype(BF16)

    akv = mm(C_KV, C_WIN)
    kv_out[0, :, 0:256] = akv[:, 0:256]
    kv_out[0, :, 256:384] = _group_norm(akv[:, 256:384], gks_ref[0], gmat)
    kv_out[0, :, 384:512] = akv[:, 384:512]

    aw = mm(C_WIN, C_ZA)
    win_out[0, :, 0:128] = _group_norm(aw[:, 0:128], gkw_ref[0], gmat)
    win_out[0, :, 128:256] = aw[:, 128:256]

    za = mm(C_ZA, C_BG)
    sa_out[0] = za * _sigmoid(za)

    gate_out[0] = _sigmoid(mm(C_GL, C_END))

    u = mm(C_CG, C_HV) * mm(C_HV, C_ZB)
    row = _row_iota(u.shape)
    r1 = pltpu.roll(u, 1, axis=0)
    r2 = pltpu.roll(u, 2, axis=0)
    if carried:
        @pl.when(pl.program_id(1) == 0)
        def _():
            carry[...] = jnp.zeros_like(carry)
        prev = carry[...]
        um1 = jnp.where(row == 0, prev[7:8], r1)
        um2 = jnp.where(row == 0, prev[6:7], jnp.where(row == 1, prev[7:8], r2))
        carry[...] = u[tm - 8:]
        u_out[0] = u[tm - 8:]
    else:
        pos = row % seq_len
        um1 = jnp.where(pos >= 1, r1, i1_ref[0])
        um2 = jnp.where(pos >= 2, r2, i2_ref[0])
        u_out[0] = u
    cw = cw_ref[0]
    y = cw[0:1] * um2 + cw[1:2] * um1 + cw[2:3] * u
    zb = mm(C_ZB, C_GL)
    cp_out[0] = (mm(C_BG, C_CG) * y * (zb * _sigmoid(zb))).astype(BF16)


def _inproj(layer, x, w_r, gn, gq_t, gks_t, gkw_t, gmat, conv_w, seq_len, tm, inits=None):
    nb, t = x.shape[:2]
    carried = seq_len >= tm
    grid = (nb, t // tm)
    tok = lambda n: pl.BlockSpec((1, tm, n), lambda b, s: (b, s, 0))
    lay2 = lambda n: pl.BlockSpec((1, 1, n), lambda b, s: (layer, 0, 0))
    in_specs = [
        tok(D_MODEL),
        lay2(D_MODEL),
        pl.BlockSpec((1, D_MODEL, C_END), lambda b, s: (layer, 0, 0)),
        lay2(512), lay2(128), lay2(128),
        pl.BlockSpec((256, 256), lambda b, s: (0, 0)),
        pl.BlockSpec((1, CONV_W, D_CONV), lambda b, s: (layer, 0, 0)),
    ]
    args = [x, gn, w_r, gq_t, gks_t, gkw_t, gmat, conv_w]
    scratch = []
    if carried:
        u_spec = pl.BlockSpec((1, 8, D_CONV), lambda b, s: (b, 0, 0))
        u_shape = (nb, 8, D_CONV)
        scratch = [pltpu.VMEM((8, D_CONV), F32)]
    else:
        in_specs += [tok(D_CONV), tok(D_CONV)]
        args += list(inits)
        u_spec = tok(D_CONV)
        u_shape = (nb, t, D_CONV)
    out_shape = [
        jax.ShapeDtypeStruct((nb, t, 512), BF16),
        jax.ShapeDtypeStruct((nb, t, 512), F32),
        jax.ShapeDtypeStruct((nb, t, 256), F32),
        jax.ShapeDtypeStruct((nb, t, 128), F32),
        jax.ShapeDtypeStruct((nb, t, 512), F32),
        jax.ShapeDtypeStruct((nb, t, 512), BF16),
        jax.ShapeDtypeStruct(u_shape, F32),
    ]
    out_specs = [tok(512), tok(512), tok(256), tok(128), tok(512), tok(512), u_spec]
    return pl.pallas_call(
        functools.partial(_inproj_kernel, seq_len, tm),
        grid=grid, in_specs=in_specs, out_specs=out_specs, out_shape=out_shape,
        scratch_shapes=scratch,
        compiler_params=pltpu.CompilerParams(
            dimension_semantics=("arbitrary", "arbitrary"), vmem_limit_bytes=VMEM_LIMIT),
        name="in_proj",
    )(*args)


def _compress_rows(load, n_half, wc_ref, pos_ref, gk_ref, gm_ref, out_ref):
    acc = jnp.zeros((2 * n_half, 256), F32)
    for l in range(CMP_BLOCK):
        xl = jnp.concatenate(
            [jnp.concatenate([load(half, l, n_half, 2 * CMP_BLOCK),
                              load(half, CMP_BLOCK + l, n_half, 2 * CMP_BLOCK)], axis=0)
             for half in range(2)], axis=1)
        xl = (xl + pos_ref[0, l:l + 1, :]).astype(BF16)
        acc = acc + jnp.dot(xl, wc_ref[0, l], preferred_element_type=F32)
    out_ref[0, :, 0:128] = _group_norm(acc[:, 0:128], gk_ref[0], gm_ref[...])
    out_ref[0, :, 128:256] = acc[:, 128:256]


def _compress_prompt_kernel(n_half, kc_ref, vc_ref, wc_ref, pos_ref, gk_ref, gm_ref, out_ref):
    refs = (kc_ref, vc_ref)
    load = lambda half, start, n, stride: refs[half][0, pl.ds(start, n, stride=stride), :]
    _compress_rows(load, n_half, wc_ref, pos_ref, gk_ref, gm_ref, out_ref)


def _compress_prompt(layer, kv, wc, posc, gk0_t, gmat):
    nb, s = kv.shape[:2]
    nc = s // CMP_BLOCK
    return pl.pallas_call(
        functools.partial(_compress_prompt_kernel, nc // 2),
        grid=(nb,),
        in_specs=[
            pl.BlockSpec((1, s, 128), lambda b: (b, 0, 0)),
            pl.BlockSpec((1, s, 128), lambda b: (b, 0, 1)),
            pl.BlockSpec((1, CMP_BLOCK, 256, 256), lambda b: (layer, 0, 0, 0)),
            pl.BlockSpec((1, CMP_BLOCK, 256), lambda b: (layer, 0, 0)),
            pl.BlockSpec((1, 1, 128), lambda b: (layer, 0, 0)),
            pl.BlockSpec((256, 256), lambda b: (0, 0)),
        ],
        out_specs=pl.BlockSpec((1, nc, 256), lambda b: (b, 0, 0)),
        out_shape=jax.ShapeDtypeStruct((nb, nc, 256), F32),
        compiler_params=pltpu.CompilerParams(
            dimension_semantics=("arbitrary",), vmem_limit_bytes=VMEM_LIMIT),
        name="compress_prompt",
    )(kv, kv, wc, posc, gk0_t, gmat)


def _topk_rows(score, blk, n):
    sel = jnp.zeros(score.shape, F32)
    for _ in range(n):
        mx = jnp.max(score, axis=0, keepdims=True)
        idx = jnp.min(jnp.where(score == mx, blk, 1 << 20), axis=0, keepdims=True)
        hit = blk == idx
        sel = jnp.where(hit, 1.0, sel)
        score = jnp.where(hit, -3e38, score)
    return sel


def _topk_lanes(score, blk, n):
    sel = jnp.zeros(score.shape, F32)
    for _ in range(n):
        mx = jnp.max(score, axis=1, keepdims=True)
        idx = jnp.min(jnp.where(score == mx, blk, 1 << 20), axis=1, keepdims=True)
        hit = blk == idx
        sel = jnp.where(hit, 1.0, sel)
        score = jnp.where(hit, -3e38, score)
    return sel


def _expand_gates(g, e_ref):
    g_hi = g.astype(BF16)
    g_lo = (g - g_hi.astype(F32)).astype(BF16)
    e = e_ref[...]
    return (jnp.dot(g_hi, e, preferred_element_type=F32)
            + jnp.dot(g_lo, e, preferred_element_type=F32))


def _attn_prompt_kernel(s_len, q_ref, kv_ref, win_ref, cmp_ref, gate_ref, sa_ref,
                        pc_ref, mk_ref, cc_ref, e_ref, o_ref,
                        ksa, vsb, kwa, vwb, kca, vcb, qa, m_scr, l_scr, acc_scr):
    qi = pl.program_id(1)
    q0 = qi * Q_TILE
    n_cmp = s_len // CMP_BLOCK
    n_blk = s_len // SEL_BLOCK

    @pl.when(qi == 0)
    def _build():
        def bld(c, carry):
            r = pl.ds(pl.multiple_of(c * 512, 512), 512)
            kvb = kv_ref[0, r, :]
            wb = win_ref[0, r, :]
            pc = pc_ref[r, :]
            lane = _lane_iota(pc.shape)
            for k in range(N_KV):
                ks = kvb[:, 0:128] if k == 0 else pltpu.roll(kvb[:, 0:128], 64, axis=1)
                kw = wb[:, 0:128] if k == 0 else pltpu.roll(wb[:, 0:128], 64, axis=1)
                ksa[k, r, 0:128] = jnp.where(lane < 64, ks, pc).astype(BF16)
                ksa[k, r, 128:256] = mk_ref[r, :]
                kwa[k, r, :] = jnp.where(lane < 64, kw, pc).astype(BF16)
            vsb[r, :] = kvb[:, 128:256].astype(BF16)
            vwb[r, :] = wb[:, 128:256].astype(BF16)
            return carry
        lax.fori_loop(0, s_len // 512, bld, 0)
        cm = cmp_ref[0]
        cc = cc_ref[...]
        lane = _lane_iota(cc.shape)
        for k in range(N_KV):
            kc = cm[:, 0:128] if k == 0 else pltpu.roll(cm[:, 0:128], 64, axis=1)
            kca[k] = jnp.where(lane < 64, kc, cc).astype(BF16)
        vcb[...] = cm[:, 128:256].astype(BF16)

    qf = q_ref[0].astype(F32)
    lane128 = _lane_iota((Q_TILE, LANES))
    trow = q0 + (_row_iota((GROUP * Q_TILE, 1)) % Q_TILE)

    o_tiles = [[None] * 4 for _ in range(3)]

    for k in range(N_KV):
        for h in range(GROUP):
            hg = k * GROUP + h
            tile = qf[:, (hg // 2) * 128:(hg // 2 + 1) * 128]
            src = tile if hg % 2 == 0 else pltpu.roll(tile, 64, axis=1)
            m = _slope(hg)
            bias = jnp.where(lane128 == 64, 64.0 * m,
                             jnp.where(lane128 == 65, m,
                                       jnp.where(lane128 == 66, -m * q0.astype(F32), 0.0)))
            qa[k, h * Q_TILE:(h + 1) * Q_TILE, 0:128] = jnp.where(lane128 < 64, src, bias).astype(BF16)

        st = lax.dot_general(kca[k], qa[k, :, 0:128], _NT, preferred_element_type=F32)
        r = _row_iota(st.shape)
        cend = CMP_BLOCK * (2 * (r % (n_cmp // 2)) + r // (n_cmp // 2)) + (CMP_BLOCK - 1)
        tq = q0 + (_lane_iota(st.shape) % Q_TILE)
        valid = cend <= tq
        st = jnp.where(valid, st, NEG)
        mx = jnp.max(st, axis=0, keepdims=True)
        e = jnp.where(valid, jnp.exp(st - mx), 0.0)
        pt = e / jnp.maximum(jnp.sum(e, axis=0, keepdims=True), 1e-30)

        imp = pt[:, 0:128] + pt[:, 128:256] + pt[:, 256:384] + pt[:, 384:512]
        imp = imp[0:n_blk] + imp[n_blk:2 * n_blk]
        blk = _row_iota(imp.shape)
        tcol = q0 + _lane_iota(imp.shape)
        cur = tcol // SEL_BLOCK
        forced = (blk == 0) | (blk == cur) | (blk == cur - 1)
        score = jnp.where(blk * SEL_BLOCK <= tcol,
                          imp + jnp.where(forced, FORCE_BONUS, 0.0), INVALID)
        sel = _topk_rows(score, blk, min(TOP_N, n_blk))
        notsel = jnp.concatenate([1.0 - sel, jnp.zeros((LANES - n_blk, Q_TILE), F32)], axis=0)
        notsel = notsel.T.astype(BF16)
        for h in range(GROUP):
            qa[k, h * Q_TILE:(h + 1) * Q_TILE, 128:256] = notsel

        oc = []
        for h in range(GROUP):
            p_h = pt[:, h * Q_TILE:(h + 1) * Q_TILE].T.astype(BF16)
            oc.append(jnp.dot(p_h, vcb[...], preferred_element_type=F32))

        m_scr[...] = jnp.full(m_scr.shape, -3e38, F32)
        l_scr[...] = jnp.zeros(l_scr.shape, F32)
        acc_scr[...] = jnp.zeros(acc_scr.shape, F32)

        def sel_body(c, carry):
            off = pl.multiple_of(c * SEL_CHUNK, SEL_CHUNK)
            kblk = ksa[k, pl.ds(off, SEL_CHUNK), :]
            s = lax.dot_general(qa[k], kblk, _NT, preferred_element_type=F32)
            kpos = off + _lane_iota(s.shape)
            s = jnp.where(kpos <= trow, s, NEG)
            m_old = m_scr[...]
            m_new = jnp.maximum(m_old, jnp.max(s, axis=1, keepdims=True))
            alpha = jnp.exp(m_old - m_new)
            p = jnp.exp(s - m_new)
            l_scr[...] = alpha * l_scr[...] + jnp.sum(p, axis=1, keepdims=True)
            acc_scr[...] = alpha * acc_scr[...] + jnp.dot(
                p.astype(BF16), vsb[pl.ds(off, SEL_CHUNK), :], preferred_element_type=F32)
            m_scr[...] = m_new
            return carry
        lax.fori_loop(0, (q0 + Q_TILE + SEL_CHUNK - 1) // SEL_CHUNK, sel_body, 0)
        osel = acc_scr[...] / l_scr[...]

        w_keys = WINDOW + Q_TILE
        start = pl.multiple_of(jnp.maximum(q0 - WINDOW, 0), Q_TILE)
        sw = lax.dot_general(qa[k, :, 0:128], kwa[k, pl.ds(start, w_keys), :], _NT,
                             preferred_element_type=F32)
        dist = trow - (start + _lane_iota(sw.shape))
        sw = jnp.where(dist.astype(jnp.uint32) < WINDOW, sw, NEG)
        mw = jnp.max(sw, axis=1, keepdims=True)
        pw = jnp.exp(sw - mw)
        lw = jnp.sum(pw, axis=1, keepdims=True)
        ow = jnp.dot(pw.astype(BF16), vwb[pl.ds(start, w_keys), :],
                     preferred_element_type=F32) / lw

        for pair in range(GROUP // 2):
            h0, h1 = 2 * pair, 2 * pair + 1
            rows = lambda a, h: a[h * Q_TILE:(h + 1) * Q_TILE]
            o_tiles[0][k * 2 + pair] = _place_pair(oc[h0], oc[h1], k)
            o_tiles[1][k * 2 + pair] = _place_pair(rows(osel, h0), rows(osel, h1), k)
            o_tiles[2][k * 2 + pair] = _place_pair(rows(ow, h0), rows(ow, h1), k)

    ge = _expand_gates(gate_ref[0], e_ref)
    sa = sa_ref[0]
    for t in range(4):
        sl = slice(t * 128, (t + 1) * 128)
        o = (ge[:, sl] * o_tiles[0][t] + ge[:, 512 + t * 128:512 + (t + 1) * 128] * o_tiles[1][t]
             + ge[:, 1024 + t * 128:1024 + (t + 1) * 128] * o_tiles[2][t])
        o_ref[0, :, sl] = (o * sa[:, sl]).astype(BF16)


def _attn_prompt(q, kv, win, cmpk, gates, sa, pc, mk, cc, emat):
    nb, s = q.shape[:2]
    nq = s // Q_TILE
    n_cmp = s // CMP_BLOCK
    tokq = lambda n: pl.BlockSpec((1, Q_TILE, n), lambda b, i: (b, i, 0))
    const = lambda a: pl.BlockSpec(a.shape, lambda b, i: (0,) * a.ndim)
    return pl.pallas_call(
        functools.partial(_attn_prompt_kernel, s),
        grid=(nb, nq),
        in_specs=[
            tokq(512),
            pl.BlockSpec((1, s, 256), lambda b, i: (b, 0, 1)),
            pl.BlockSpec((1, s, 256), lambda b, i: (b, 0, 0)),
            pl.BlockSpec((1, n_cmp, 256), lambda b, i: (b, 0, 0)),
            tokq(128), tokq(512),
            const(pc), const(mk), const(cc), const(emat),
        ],
        out_specs=tokq(512),
        out_shape=jax.ShapeDtypeStruct((nb, s, 512), BF16),
        scratch_shapes=[
            pltpu.VMEM((N_KV, s, 256), BF16),
            pltpu.VMEM((s, 128), BF16),
            pltpu.VMEM((N_KV, s, 128), BF16),
            pltpu.VMEM((s, 128), BF16),
            pltpu.VMEM((N_KV, n_cmp, 128), BF16),
            pltpu.VMEM((n_cmp, 128), BF16),
            pltpu.VMEM((N_KV, GROUP * Q_TILE, 256), BF16),
            pltpu.VMEM((GROUP * Q_TILE, 1), F32),
            pltpu.VMEM((GROUP * Q_TILE, 1), F32),
            pltpu.VMEM((GROUP * Q_TILE, 128), F32),
        ],
        compiler_params=pltpu.CompilerParams(
            dimension_semantics=("arbitrary", "arbitrary"), vmem_limit_bytes=VMEM_LIMIT),
        name="attn_prompt",
    )(q, kv, win, cmpk, gates, sa, pc, mk, cc, emat)


V_EXT = HEAD_DIM + 16


def _masked_exp(st, valid):
    st = jnp.where(valid, st, NEG)
    mx = jnp.max(st, axis=0, keepdims=True)
    return jnp.exp(st - mx), mx


def _normalize_ext(o_ext):
    return o_ext[0:HEAD_DIM] * (1.0 / o_ext[HEAD_DIM:HEAD_DIM + 1])


def _attn_prompt_t_kernel(s_len, q_ref, kv_ref, win_ref, cmp_ref, gate_ref, sa_ref,
                          pc_ref, mk_ref, cc_ref, o_ref,
                          ksa, vst, kwa, vwt, kca, vct, qa, acc_scr):
    qi = pl.program_id(1)
    q0 = qi * Q_TILE
    n_cmp = s_len // CMP_BLOCK
    n_blk = s_len // SEL_BLOCK
    kt = SEL_CHUNK // LANES
    w_keys = WINDOW + Q_TILE
    wt = w_keys // LANES

    @pl.when(qi == 0)
    def _build():
        def bld(c, carry):
            r = pl.ds(pl.multiple_of(c * 512, 512), 512)
            kvb = kv_ref[0, r, :]
            wb = win_ref[0, r, :]
            pc = pc_ref[r, :]
            lane = _lane_iota(pc.shape)
            for k in range(N_KV):
                ks = kvb[:, 0:128] if k == 0 else pltpu.roll(kvb[:, 0:128], 64, axis=1)
                kw = wb[:, 0:128] if k == 0 else pltpu.roll(wb[:, 0:128], 64, axis=1)
                ksa[k, r, 0:128] = jnp.where(lane < 64, ks, pc).astype(BF16)
                ksa[k, r, 128:256] = mk_ref[r, :]
                kwa[k, r, :] = jnp.where(lane < 64, kw, pc).astype(BF16)
            ones = jnp.ones((V_EXT - HEAD_DIM, LANES), BF16)
            for j in range(512 // LANES):
                rows = slice(j * LANES, (j + 1) * LANES)
                for dst, src in ((vst, kvb), (vwt, wb)):
                    t = src[rows, 128:256].T.astype(BF16)
                    tile = jnp.concatenate([t[0:HEAD_DIM], ones, t[HEAD_DIM:], ones], axis=0)
                    dst[c * (512 // LANES) + j] = tile
            return carry
        lax.fori_loop(0, s_len // 512, bld, 0)
        cm = cmp_ref[0]
        cc = cc_ref[...]
        lane = _lane_iota(cc.shape)
        for k in range(N_KV):
            kc = cm[:, 0:128] if k == 0 else pltpu.roll(cm[:, 0:128], 64, axis=1)
            kca[k] = jnp.where(lane < 64, kc, cc).astype(BF16)
        vct[...] = cm[:, 128:256].T.astype(BF16)

    qf = q_ref[0].astype(F32)
    lane128 = _lane_iota((Q_TILE, LANES))
    hq = GROUP * Q_TILE
    tq = q0 + (_lane_iota((1, hq)) % Q_TILE)

    for k in range(N_KV):
        for h in range(GROUP):
            hg = k * GROUP + h
            tile = qf[:, (hg // 2) * 128:(hg // 2 + 1) * 128]
            src = tile if hg % 2 == 0 else pltpu.roll(tile, 64, axis=1)
            m = _slope(hg)
            bias = jnp.where(lane128 == 64, 64.0 * m,
                             jnp.where(lane128 == 65, m,
                                       jnp.where(lane128 == 66, -m * q0.astype(F32), 0.0)))
            qa[k, h * Q_TILE:(h + 1) * Q_TILE, 0:128] = jnp.where(lane128 < 64, src, bias).astype(BF16)

    oct_ = []
    for k in range(N_KV):
        dims = slice(k * HEAD_DIM, (k + 1) * HEAD_DIM)
        st = lax.dot_general(kca[k], qa[k, :, 0:128], _NT, preferred_element_type=F32)
        r = _row_iota(st.shape)
        cend = CMP_BLOCK * (2 * (r % (n_cmp // 2)) + r // (n_cmp // 2)) + (CMP_BLOCK - 1)
        e, mx = _masked_exp(st, cend <= tq)
        den = jnp.maximum(jnp.sum(e, axis=0, keepdims=True), 1e-30)
        pt = e * jnp.where(mx > 0.5 * NEG, 1.0 / den, 0.0)
        oct_.append(jnp.dot(vct[dims, :], pt.astype(BF16), preferred_element_type=F32))

        imp = pt[:, 0:128] + pt[:, 128:256] + pt[:, 256:384] + pt[:, 384:512]
        imp = imp[0:n_blk] + imp[n_blk:2 * n_blk]
        blk = _row_iota(imp.shape)
        tcol = q0 + _lane_iota(imp.shape)
        cur = tcol // SEL_BLOCK
        forced = (blk == 0) | (blk == cur) | (blk == cur - 1)
        score = jnp.where(blk * SEL_BLOCK <= tcol,
                          imp + jnp.where(forced, FORCE_BONUS, 0.0), INVALID)
        sel = _topk_rows(score, blk, min(TOP_N, n_blk))
        notsel = jnp.concatenate([1.0 - sel, jnp.zeros((LANES - n_blk, Q_TILE), F32)], axis=0)
        notsel = notsel.T.astype(BF16)
        for h in range(GROUP):
            qa[k, h * Q_TILE:(h + 1) * Q_TILE, 128:256] = notsel

    start = pl.multiple_of(jnp.maximum(q0 - WINDOW, 0), Q_TILE)
    owt = []
    for k in range(N_KV):
        sw = lax.dot_general(kwa[k, pl.ds(start, w_keys), :], qa[k, :, 0:128], _NT,
                             preferred_element_type=F32)
        dist = tq - (start + _row_iota(sw.shape))
        pw, _ = _masked_exp(sw, dist.astype(jnp.uint32) < WINDOW)
        vw = vwt[pl.ds(start // LANES, wt), k * V_EXT:(k + 1) * V_EXT, :]
        vw = jnp.concatenate([vw[j] for j in range(wt)], axis=1)
        owt.append(_normalize_ext(jnp.dot(vw, pw.astype(BF16), preferred_element_type=F32)))

    def scores(c):
        off = pl.multiple_of(c * SEL_CHUNK, SEL_CHUNK)
        return tuple(lax.dot_general(ksa[k, pl.ds(off, SEL_CHUNK), :], qa[k], _NT,
                                     preferred_element_type=F32) for k in range(N_KV))

    def consume(c, s_all, m_all, masked):
        off = pl.multiple_of(c * SEL_CHUNK, SEL_CHUNK)
        out = []
        for k in range(N_KV):
            s = s_all[k]
            if masked:
                s = jnp.where(off + _row_iota(s.shape) <= tq, s, NEG)
            m_new = jnp.maximum(m_all[k], jnp.max(s, axis=0, keepdims=True))
            alpha = jnp.exp(m_all[k] - m_new)
            p = jnp.exp(s - m_new)
            v = vst[pl.ds(c * kt, kt), k * V_EXT:(k + 1) * V_EXT, :]
            v = jnp.concatenate([v[j] for j in range(kt)], axis=1)
            acc_scr[k] = alpha * acc_scr[k] + jnp.dot(v, p.astype(BF16),
                                                      preferred_element_type=F32)
            out.append(m_new)
        return tuple(out)

    acc_scr[...] = jnp.zeros(acc_scr.shape, F32)
    init = (jnp.full((1, hq), -3e38, F32),) * N_KV
    n_full = q0 // SEL_CHUNK

    def body(c, carry):
        s_next = scores(c + 1)
        return s_next + consume(c, carry[:N_KV], carry[N_KV:], False)
    carry = lax.fori_loop(0, n_full, body, scores(0) + init)
    consume(n_full, carry[:N_KV], carry[N_KV:], True)

    gt = gate_ref[0].T
    sa = sa_ref[0]
    for k in range(N_KV):
        ost = _normalize_ext(acc_scr[k])
        owk = owt[k]
        for pair in range(GROUP // 2):
            halves = []
            for h in (2 * pair, 2 * pair + 1):
                hg = k * GROUP + h
                cols = slice(h * Q_TILE, (h + 1) * Q_TILE)
                halves.append(gt[3 * hg:3 * hg + 1] * oct_[k][:, cols]
                              + gt[3 * hg + 1:3 * hg + 2] * ost[:, cols]
                              + gt[3 * hg + 2:3 * hg + 3] * owk[:, cols])
            t = k * 2 + pair
            o = jnp.concatenate(halves, axis=0).T
            o_ref[0, :, t * 128:(t + 1) * 128] = (o * sa[:, t * 128:(t + 1) * 128]).astype(BF16)


def _attn_prompt_t(q, kv, win, cmpk, gates, sa, pc, mk, cc):
    nb, s = q.shape[:2]
    nq = s // Q_TILE
    n_cmp = s // CMP_BLOCK
    tokq = lambda n: pl.BlockSpec((1, Q_TILE, n), lambda b, i: (b, i, 0))
    const = lambda a: pl.BlockSpec(a.shape, lambda b, i: (0,) * a.ndim)
    return pl.pallas_call(
        functools.partial(_attn_prompt_t_kernel, s),
        grid=(nb, nq),
        in_specs=[
            tokq(512),
            pl.BlockSpec((1, s, 256), lambda b, i: (b, 0, 1)),
            pl.BlockSpec((1, s, 256), lambda b, i: (b, 0, 0)),
            pl.BlockSpec((1, n_cmp, 256), lambda b, i: (b, 0, 0)),
            tokq(128), tokq(512),
            const(pc), const(mk), const(cc),
        ],
        out_specs=tokq(512),
        out_shape=jax.ShapeDtypeStruct((nb, s, 512), BF16),
        scratch_shapes=[
            pltpu.VMEM((N_KV, s, 256), BF16),
            pltpu.VMEM((s // LANES, N_KV * V_EXT, LANES), BF16),
            pltpu.VMEM((N_KV, s, 128), BF16),
            pltpu.VMEM((s // LANES, N_KV * V_EXT, LANES), BF16),
            pltpu.VMEM((N_KV, n_cmp, 128), BF16),
            pltpu.VMEM((128, n_cmp), BF16),
            pltpu.VMEM((N_KV, GROUP * Q_TILE, 256), BF16),
            pltpu.VMEM((N_KV, V_EXT, GROUP * Q_TILE), F32),
        ],
        compiler_params=pltpu.CompilerParams(
            dimension_semantics=("arbitrary", "arbitrary"), vmem_limit_bytes=VMEM_LIMIT),
        name="attn_prompt",
    )(q, kv, win, cmpk, gates, sa, pc, mk, cc)


def _outproj_kernel(x_ref, oa_ref, cp_ref, p_ref, wo_ref, wple_ref, wpg_ref, g_ref, y_ref):
    x1 = (x_ref[0]
          + jnp.dot(oa_ref[0], wo_ref[0, 0:D_ATT, :], preferred_element_type=F32)
          + jnp.dot(cp_ref[0], wo_ref[0, D_ATT:, :], preferred_element_type=F32))
    ms = jnp.mean(x1 * x1, axis=-1, keepdims=True)
    h2 = (x1 * lax.rsqrt(ms + EPS) * g_ref[0]).astype(BF16)
    gate = _sigmoid(jnp.dot(h2, wpg_ref[0], preferred_element_type=F32))
    pe = jnp.dot(p_ref[0, 0].astype(BF16), wple_ref[0], preferred_element_type=F32)
    y_ref[0] = x1 + pe * gate


def _outproj(layer, x, oa, cp, p_all, wo, wple, wpg, g_ple, tm):
    nb, t = x.shape[:2]
    tok = lambda n: pl.BlockSpec((1, tm, n), lambda b, s: (b, s, 0))
    return pl.pallas_call(
        _outproj_kernel,
        grid=(nb, t // tm),
        in_specs=[
            tok(D_MODEL), tok(512), tok(512),
            pl.BlockSpec((1, 1, tm, D_PLE), lambda b, s: (layer, b, s, 0)),
            pl.BlockSpec((1, D_MODEL, D_MODEL), lambda b, s: (layer, 0, 0)),
            pl.BlockSpec((1, D_PLE, D_MODEL), lambda b, s: (layer, 0, 0)),
            pl.BlockSpec((1, D_MODEL, D_MODEL), lambda b, s: (layer, 0, 0)),
            pl.BlockSpec((1, 1, D_MODEL), lambda b, s: (layer, 0, 0)),
        ],
        out_specs=tok(D_MODEL),
        out_shape=jax.ShapeDtypeStruct((nb, t, D_MODEL), F32),
        compiler_params=pltpu.CompilerParams(
            dimension_semantics=("arbitrary", "arbitrary"), vmem_limit_bytes=VMEM_LIMIT),
        name="out_proj",
    )(x, oa, cp, p_all, wo, wple, wpg, g_ple)


def _page_copy(cache_ref, pt_ref, buf, sem, b, slot, n_pages, row0):
    return [pltpu.make_async_copy(
        cache_ref.at[pt_ref[b, p], pl.ds(row0, 256), :],
        buf.at[slot, pl.ds(p * 256, 256), :],
        sem.at[slot]) for p in range(n_pages)]


def _paged_prefetch(cache_ref, pt_ref, buf, sem, n_pages, row0):
    b = pl.program_id(0)
    slot = b % 2

    @pl.when(b == 0)
    def _():
        for cp in _page_copy(cache_ref, pt_ref, buf, sem, 0, 0, n_pages, row0):
            cp.start()

    @pl.when(b + 1 < pl.num_programs(0))
    def _():
        for cp in _page_copy(cache_ref, pt_ref, buf, sem, b + 1, 1 - slot, n_pages, row0):
            cp.start()

    for cp in _page_copy(cache_ref, pt_ref, buf, sem, b, slot, n_pages, row0):
        cp.wait()
    return slot


def _compress_sample_kernel(n_pages, pt_ref, cache_ref, wc_ref, post_ref, perm_ref, gk_ref, gm_ref,
                            out_ref, buf, sem, zbuf):
    slot = _paged_prefetch(cache_ref, pt_ref, buf, sem, n_pages, 0)
    per_pair = 2 * PAGE_SIZE // CMP_BLOCK

    def regroup(g, carry):
        xt = jnp.concatenate(
            [buf[slot, pl.ds(pl.multiple_of((2 * g + j) * 256, 256), 256), :] for j in range(2)],
            axis=1)
        xt = (xt + post_ref[0]).astype(BF16)
        z = lax.dot_general(perm_ref[...], xt, _NT, preferred_element_type=F32)
        rows = pl.ds(pl.multiple_of(g * per_pair, per_pair), per_pair)
        for l in range(CMP_BLOCK):
            zbuf[l, rows, :] = z[l * per_pair:(l + 1) * per_pair]

    def regroup4(g4, carry):
        for j in range(4):
            regroup(4 * g4 + j, carry)
        return carry
    lax.fori_loop(0, n_pages // 8, regroup4, 0)

    acc = jnp.zeros((zbuf.shape[1], 256), F32)
    for l in range(CMP_BLOCK):
        acc = acc + jnp.dot(zbuf[l].astype(BF16), wc_ref[0, l], preferred_element_type=F32)
    out_ref[0, :, 0:128] = _group_norm(acc[:, 0:128], gk_ref[0], gm_ref[...])
    out_ref[0, :, 128:256] = acc[:, 128:256]


def _compress_sample(layer, pt, cache3, wc, post, perm, gk0_t, gmat):
    nb, n_pages = pt.shape
    nc = n_pages * PAGE_SIZE // CMP_BLOCK
    grid_spec = pltpu.PrefetchScalarGridSpec(
        num_scalar_prefetch=1,
        grid=(nb,),
        in_specs=[
            pl.BlockSpec(memory_space=pl.ANY),
            pl.BlockSpec((1, CMP_BLOCK, 256, 256), lambda b, pt: (layer, 0, 0, 0)),
            pl.BlockSpec((1, 256, 2 * PAGE_SIZE), lambda b, pt: (layer, 0, 0)),
            pl.BlockSpec((2 * PAGE_SIZE, 2 * PAGE_SIZE), lambda b, pt: (0, 0)),
            pl.BlockSpec((1, 1, 128), lambda b, pt: (layer, 0, 0)),
            pl.BlockSpec((256, 256), lambda b, pt: (0, 0)),
        ],
        out_specs=pl.BlockSpec((1, nc, 256), lambda b, pt: (b, 0, 0)),
        scratch_shapes=[
            pltpu.VMEM((2, n_pages * 256, PAGE_SIZE), F32),
            pltpu.SemaphoreType.DMA((2,)),
            pltpu.VMEM((CMP_BLOCK, nc, 256), F32),
        ],
    )
    return pl.pallas_call(
        functools.partial(_compress_sample_kernel, n_pages),
        grid_spec=grid_spec,
        out_shape=jax.ShapeDtypeStruct((nb, nc, 256), F32),
        compiler_params=pltpu.CompilerParams(
            dimension_semantics=("arbitrary",), vmem_limit_bytes=VMEM_LIMIT),
        name="compress_sample",
    )(pt, cache3, wc, post, perm, gk0_t, gmat)


def _attn_sample_kernel(n_pages, dec_len, pt_ref, cache_ref, q_ref, kvn_ref, wn_ref,
                        cwin_ref, cmp_ref, gate_ref, sa_ref, mk_ref, e_ref, pair_ref, o_ref,
                        buf, sem, kta, vtb):
    past = n_pages * PAGE_SIZE
    n_cmp = past // CMP_BLOCK
    n_blk = past // SEL_BLOCK
    rows = N_KV * GROUP * dec_len
    hrows = GROUP * dec_len

    @pl.when(pl.program_id(0) == 0)
    def _():
        kta[128:256, :] = mk_ref[...]

    slot = _paged_prefetch(cache_ref, pt_ref, buf, sem, n_pages, 256)
    for p in range(n_pages):
        blk = buf[slot, p * 256:(p + 1) * 256, :]
        kta[0:128, p * PAGE_SIZE:(p + 1) * PAGE_SIZE] = blk[0:128].astype(BF16)
        vtb[:, p * PAGE_SIZE:(p + 1) * PAGE_SIZE] = blk[128:256].astype(BF16)

    qf = q_ref[0].astype(F32)
    lane = _lane_iota((dec_len, LANES))
    pieces = []
    for k in range(N_KV):
        for h in range(GROUP):
            hg = k * GROUP + h
            tile = qf[:, (hg // 2) * 128:(hg // 2 + 1) * 128]
            src = tile if (hg % 2) == k else pltpu.roll(tile, 64, axis=1)
            keep = (lane < 64) if k == 0 else (lane >= 64)
            pieces.append(jnp.where(keep, src, 0.0))
    qbd = jnp.concatenate(pieces, axis=0)
    qbd16 = qbd.astype(BF16)

    rid = _row_iota((rows, 1))
    slope = jnp.zeros((rows, 1), F32)
    for hg in range(N_HEADS):
        slope = jnp.where(rid // dec_len == hg, _slope(hg), slope)
    qpos = rid % dec_len

    cm = cmp_ref[0]
    sc = lax.dot_general(qbd16, cm[:, 0:128].astype(BF16), _NT, preferred_element_type=F32)
    cend = CMP_BLOCK * _lane_iota(sc.shape) + (CMP_BLOCK - 1)
    sc = sc + slope * (cend - past).astype(F32)
    mc = jnp.max(sc, axis=1, keepdims=True)
    ec = jnp.exp(sc - mc)
    pcm = ec / jnp.maximum(jnp.sum(ec, axis=1, keepdims=True), 1e-30)
    ocm = jnp.dot(pcm.astype(BF16), cm[:, 128:256].astype(BF16), preferred_element_type=F32)

    scores = []
    for k in range(N_KV):
        imp = pcm[k * hrows:k * hrows + dec_len]
        for h in range(1, GROUP):
            imp = imp + pcm[k * hrows + h * dec_len:k * hrows + (h + 1) * dec_len]
        hi = imp.astype(BF16)
        mid = (imp - hi.astype(F32)).astype(BF16)
        lo = (imp - hi.astype(F32) - mid.astype(F32)).astype(BF16)
        imp = sum(jnp.dot(part, pair_ref[...], preferred_element_type=F32)
                  for part in (hi, mid, lo))
        blk = _lane_iota(imp.shape)
        t = past + _row_iota(imp.shape)
        cur = t // SEL_BLOCK
        forced = (blk == 0) | (blk == cur) | (blk == cur - 1)
        score = imp + jnp.where(forced, FORCE_BONUS, 0.0)
        if n_blk < LANES:
            score = jnp.concatenate([score, jnp.full((dec_len, LANES - n_blk), -3e38, F32)], axis=1)
        scores.append(score)
    score_t = jnp.concatenate(
        scores + [jnp.zeros((LANES - N_KV * dec_len, LANES), F32)], axis=0).T
    sel_t = _topk_rows(score_t, _row_iota(score_t.shape), TOP_N - 1)
    ns = 1.0 - sel_t.T
    notsel = jnp.concatenate(
        [ns[k * dec_len:(k + 1) * dec_len] for k in range(N_KV) for _ in range(GROUP)], axis=0)
    qaug = jnp.concatenate([qbd16, notsel.astype(BF16)], axis=1)

    s1 = jnp.dot(qaug, kta[...], preferred_element_type=F32)
    s1 = s1 + slope * (_lane_iota(s1.shape) - past).astype(F32)
    kn = jnp.concatenate([kvn_ref[0][:, 256:384], jnp.zeros((LANES - dec_len, 128), F32)], axis=0)
    vn = jnp.concatenate([kvn_ref[0][:, 384:512], jnp.zeros((LANES - dec_len, 128), F32)], axis=0)
    s2 = lax.dot_general(qbd16, kn.astype(BF16), _NT, preferred_element_type=F32)
    j2 = _lane_iota(s2.shape)
    s2 = jnp.where(j2 <= qpos, s2 + slope * j2.astype(F32), NEG)
    ms = jnp.maximum(jnp.max(s1, axis=1, keepdims=True), jnp.max(s2, axis=1, keepdims=True))
    p1 = jnp.exp(s1 - ms)
    p2 = jnp.exp(s2 - ms)
    ls = jnp.sum(p1, axis=1, keepdims=True) + jnp.sum(p2, axis=1, keepdims=True)
    osl = (lax.dot_general(p1.astype(BF16), vtb[...], _NT, preferred_element_type=F32)
           + jnp.dot(p2.astype(BF16), vn.astype(BF16), preferred_element_type=F32)) / ls

    cw = cwin_ref[0, 0]
    wb = cw.shape[1]
    w1 = jnp.dot(qbd16, cw[0:128].astype(BF16), preferred_element_type=F32)
    jw = _lane_iota(w1.shape)
    dist = qpos + (wb - jw)
    w1 = jnp.where(dist < WINDOW, w1 + slope * (jw - wb).astype(F32), NEG)
    kwn = jnp.concatenate([wn_ref[0][:, 0:128], jnp.zeros((LANES - dec_len, 128), F32)], axis=0)
    vwn = jnp.concatenate([wn_ref[0][:, 128:256], jnp.zeros((LANES - dec_len, 128), F32)], axis=0)
    w2 = lax.dot_general(qbd16, kwn.astype(BF16), _NT, preferred_element_type=F32)
    w2 = jnp.where(j2 <= qpos, w2 + slope * j2.astype(F32), NEG)
    mw = jnp.maximum(jnp.max(w1, axis=1, keepdims=True), jnp.max(w2, axis=1, keepdims=True))
    pw1 = jnp.exp(w1 - mw)
    pw2 = jnp.exp(w2 - mw)
    lw = jnp.sum(pw1, axis=1, keepdims=True) + jnp.sum(pw2, axis=1, keepdims=True)
    owd = (lax.dot_general(pw1.astype(BF16), cw[128:256].astype(BF16), _NT,
                           preferred_element_type=F32)
           + jnp.dot(pw2.astype(BF16), vwn.astype(BF16), preferred_element_type=F32)) / lw

    ge = _expand_gates(gate_ref[0], e_ref)
    sa = sa_ref[0]
    for t4 in range(4):
        k, pair = t4 // 2, t4 % 2
        r0 = (k * GROUP + 2 * pair) * dec_len
        r1 = r0 + dec_len
        rows_of = lambda a, r: a[r:r + dec_len]
        sl = slice(t4 * 128, (t4 + 1) * 128)
        o = (ge[:, sl] * _place_pair(rows_of(ocm, r0), rows_of(ocm, r1), k)
             + ge[:, 512 + t4 * 128:512 + (t4 + 1) * 128]
             * _place_pair(rows_of(osl, r0), rows_of(osl, r1), k)
             + ge[:, 1024 + t4 * 128:1024 + (t4 + 1) * 128]
             * _place_pair(rows_of(owd, r0), rows_of(owd, r1), k))
        o_ref[0, :, sl] = (o * sa[:, sl]).astype(BF16)


def _attn_sample(layer, pt, cache3, q, kvn, wn, cwin, cmpk, gates, sa, mk, emat, pairm):
    nb, n_pages = pt.shape
    dec_len = q.shape[1]
    past = n_pages * PAGE_SIZE
    wb = cwin.shape[3]
    tok = lambda n: pl.BlockSpec((1, dec_len, n), lambda b, pt: (b, 0, 0))
    grid_spec = pltpu.PrefetchScalarGridSpec(
        num_scalar_prefetch=1,
        grid=(nb,),
        in_specs=[
            pl.BlockSpec(memory_space=pl.ANY),
            tok(512), tok(512), tok(256),
            pl.BlockSpec((1, 1, 256, wb), lambda b, pt: (layer, b, 0, 0)),
            pl.BlockSpec((1, past // CMP_BLOCK, 256), lambda b, pt: (b, 0, 0)),
            tok(128), tok(512),
            pl.BlockSpec(mk.shape, lambda b, pt: (0, 0)),
            pl.BlockSpec(emat.shape, lambda b, pt: (0, 0)),
            pl.BlockSpec(pairm.shape, lambda b, pt: (0, 0)),
        ],
        out_specs=tok(512),
        scratch_shapes=[
            pltpu.VMEM((2, n_pages * 256, PAGE_SIZE), F32),
            pltpu.SemaphoreType.DMA((2,)),
            pltpu.VMEM((256, past), BF16),
            pltpu.VMEM((128, past), BF16),
        ],
    )
    return pl.pallas_call(
        functools.partial(_attn_sample_kernel, n_pages, dec_len),
        grid_spec=grid_spec,
        out_shape=jax.ShapeDtypeStruct((nb, dec_len, 512), BF16),
        compiler_params=pltpu.CompilerParams(
            dimension_semantics=("arbitrary",), vmem_limit_bytes=VMEM_LIMIT),
        name="attn_sample",
    )(pt, cache3, q, kvn, wn, cwin, cmpk, gates, sa, mk, emat, pairm)


def _key_consts(n_keys, n_cmp):
    kpos = np.arange(n_keys)
    pc = np.zeros((n_keys, LANES), np.float32)
    pc[:, 64] = kpos // 64
    pc[:, 65] = kpos % 64
    pc[:, 66] = 1.0
    mk = np.zeros((n_keys, LANES), np.float32)
    mk[kpos, (kpos // SEL_BLOCK) % LANES] = -BIG
    r = np.arange(n_cmp)
    cend = CMP_BLOCK * (2 * (r % (n_cmp // 2)) + r // (n_cmp // 2)) + CMP_BLOCK - 1
    cc = np.zeros((n_cmp, LANES), np.float32)
    cc[:, 64] = cend // 64
    cc[:, 65] = cend % 64
    cc[:, 66] = 1.0
    return jnp.asarray(pc), jnp.asarray(mk, dtype=BF16), jnp.asarray(cc)


def _block_mask_rows(n_keys):
    kpos = np.arange(n_keys)
    mk = np.zeros((LANES, n_keys), np.float32)
    mk[(kpos // SEL_BLOCK) % LANES, kpos] = -BIG
    return jnp.asarray(mk, dtype=BF16)


def _gate_expand_matrix():
    e = np.zeros((LANES, 3 * D_ATT), np.float32)
    for h in range(N_HEADS):
        for j in range(3):
            e[h * 3 + j, j * D_ATT + h * HEAD_DIM:j * D_ATT + (h + 1) * HEAD_DIM] = 1.0
    return jnp.asarray(e, dtype=BF16)


def _group_mean_matrix():
    i = np.arange(256)
    return jnp.asarray((i[:, None] // HEAD_DIM == i[None, :] // HEAD_DIM) / HEAD_DIM, dtype=BF16)


def _reorder_w_in(w_in):
    o = np.cumsum([0, 512, 128, 128, 128, 128, 128, 128, 24, 512, 512, 512, 512, 512])
    q, kc, vc, ks, vs, kw, vw, gl, za, bg, cg, hv, zb = [
        w_in[..., o[i]:o[i + 1]] for i in range(13)]
    pad = jnp.zeros(w_in.shape[:-1] + (LANES - 24,), w_in.dtype)
    return jnp.concatenate([q, kc, vc, ks, vs, kw, vw, za, bg, cg, hv, zb, gl, pad],
                           axis=-1).astype(BF16)


def _compress_weights(w_phi, cmp_pos):
    w16 = w_phi.astype(BF16)
    zero = jnp.zeros_like(w16[:, 0])
    blocks = [w16[:, 0], w16[:, 0], w16[:, 1], w16[:, 1]]
    wc = jnp.concatenate(
        [jnp.concatenate([blocks[i] if i == j else zero for j in range(4)], axis=-1)
         for i in range(4)], axis=-2)
    posc = jnp.concatenate([cmp_pos[:, 0], cmp_pos[:, 0], cmp_pos[:, 1], cmp_pos[:, 1]], axis=-1)
    post = jnp.tile(jnp.transpose(posc, (0, 2, 1)), (1, 1, 2 * PAGE_SIZE // CMP_BLOCK))
    return wc, posc, post


def _regroup_matrix():
    n = 2 * PAGE_SIZE
    per_pair = n // CMP_BLOCK
    p = np.zeros((n, n), np.float32)
    for l in range(CMP_BLOCK):
        for c in range(per_pair):
            p[l * per_pair + c, CMP_BLOCK * c + l] = 1.0
    return jnp.asarray(p, dtype=BF16)


def _pair_sum_matrix(n_cmp):
    p = np.zeros((n_cmp, n_cmp // 2), np.float32)
    p[np.arange(n_cmp), np.arange(n_cmp) // 2] = 1.0
    return jnp.asarray(p, dtype=BF16)


def kernel(x_prompt, x_sample, cache_kv, cache_win, state_conv, page_table, p_prompt, p_sample,
           g_norm, w_in, g_q, g_k, cmp_pos, w_phi, conv_w, w_out, w_ple, w_pg, g_ple):
    depth = w_in.shape[0]
    nb, s_len = x_prompt.shape[:2]
    db, dec_len = x_sample.shape[:2]
    n_pool = cache_kv.shape[1]
    n_pages = page_table.shape[1]
    past = n_pages * PAGE_SIZE
    wb = cache_win.shape[2]

    w_r = _reorder_w_in(w_in)
    wo16, wple16, wpg16 = w_out.astype(BF16), w_ple.astype(BF16), w_pg.astype(BF16)
    wc, posc, post = _compress_weights(w_phi, cmp_pos)
    perm = _regroup_matrix()
    pairm = _pair_sum_matrix(past // CMP_BLOCK)
    gq_t = jnp.tile(g_q, (1, N_HEADS))[:, None]
    gk0_t = jnp.tile(g_k[:, 0], (1, N_KV))[:, None]
    gks_t = jnp.tile(g_k[:, 1], (1, N_KV))[:, None]
    gkw_t = jnp.tile(g_k[:, 2], (1, N_KV))[:, None]
    g_norm = g_norm[:, None]
    g_ple = g_ple[:, None]
    gmat = _group_mean_matrix()
    emat = _gate_expand_matrix()
    pc_p, mk_p, cc_p = _key_consts(s_len, s_len // CMP_BLOCK)
    mk_s = _block_mask_rows(past)

    cache3 = jnp.transpose(cache_kv, (0, 1, 3, 4, 5, 2)).reshape(
        depth * n_pool, 4 * N_KV * HEAD_DIM, PAGE_SIZE)
    cwin4 = jnp.transpose(cache_win, (0, 1, 3, 4, 5, 2)).reshape(
        depth, db, 2 * N_KV * HEAD_DIM, wb)
    xs = x_sample.reshape(1, db * dec_len, D_MODEL)
    ps = p_sample.reshape(depth, 1, db * dec_len, D_PLE)

    xp = x_prompt
    outs = [[] for _ in range(6)]
    for i in range(depth):
        q, kv, win, gates, sa, cp, tail = _inproj(
            i, xp, w_r, g_norm, gq_t, gks_t, gkw_t, gmat, conv_w, s_len, 512)
        cmpk = _compress_prompt(i, kv, wc, posc, gk0_t, gmat)
        oa = _attn_prompt_t(q, kv, win, cmpk, gates, sa, pc_p, mk_p, cc_p)
        xp = _outproj(i, xp, oa, cp, p_prompt, wo16, wple16, wpg16, g_ple, 512)
        outs[0].append(kv.reshape(nb, s_len, 4, N_KV, HEAD_DIM))
        outs[1].append(win[:, s_len - min(WINDOW, s_len):].reshape(nb, -1, 2, N_KV, HEAD_DIM))
        outs[2].append(tail[:, 8 - (CONV_W - 1):])

        st = state_conv[i]
        pad = jnp.zeros((db, dec_len - 1, D_CONV), F32)
        init1 = jnp.concatenate([st[:, 1:2], pad], axis=1).reshape(1, db * dec_len, D_CONV)
        init2 = jnp.concatenate([st[:, 0:1], st[:, 1:2], pad[:, 1:]], axis=1).reshape(
            1, db * dec_len, D_CONV)
        q, kv, win, gates, sa, cp, u = _inproj(
            i, xs, w_r, g_norm, gq_t, gks_t, gkw_t, gmat, conv_w, dec_len, db * dec_len,
            inits=(init1, init2))
        pt = page_table + i * n_pool
        cmpk = _compress_sample(i, pt, cache3, wc, post, perm, gk0_t, gmat)
        r3 = lambda a: a.reshape(db, dec_len, a.shape[-1])
        oa = _attn_sample(i, pt, cache3, r3(q), r3(kv), r3(win), cwin4, cmpk, r3(gates), r3(sa),
                          mk_s, emat, pairm)
        xs = _outproj(i, xs, oa.reshape(1, db * dec_len, 512), cp, ps, wo16, wple16, wpg16,
                      g_ple, db * dec_len)
        outs[3].append(kv.reshape(db, dec_len, 4, N_KV, HEAD_DIM))
        new_win = jnp.concatenate(
            [cwin4[i][:, :, dec_len:], jnp.transpose(r3(win), (0, 2, 1))], axis=2)
        outs[4].append(jnp.transpose(new_win.reshape(db, 2, N_KV, HEAD_DIM, wb), (0, 4, 1, 2, 3)))
        outs[5].append(u.reshape(db, dec_len, D_CONV)[:, dec_len - (CONV_W - 1):])

    return (xp, xs.reshape(db, dec_len, D_MODEL)) + tuple(jnp.stack(o) for o in outs)
```

```python
import functools

import numpy as np
import jax
import jax.numpy as jnp
from jax import lax
from jax.experimental import pallas as pl
from jax.experimental.pallas import tpu as pltpu

D_MODEL = 1024
N_HEADS = 8
HEAD_DIM = 64
N_KV = 2
GROUP = 4
D_ATT = 512
D_CONV = 512
D_KV = 128
CMP_BLOCK = 32
SEL_BLOCK = 64
TOP_N = 16
WINDOW = 512
CONV_W = 3
D_PLE = 256
PAGE_SIZE = 128
EPS = 1e-6
NEG = -1e30
INVALID = -1e9
FORCE_BONUS = 1e3
BIG = 1e30

LANES = 128
Q_TILE = 128
SEL_CHUNK = 256
VMEM_LIMIT = 56 * 1024 * 1024

F32 = jnp.float32
BF16 = jnp.bfloat16

C_Q, C_KV, C_WIN, C_ZA, C_BG, C_CG, C_HV, C_ZB, C_GL, C_END = (
    0, 512, 1024, 1280, 1792, 2304, 2816, 3328, 3840, 3968)

_NT = (((1,), (1,)), ((), ()))


def _slope(h):
    return 2.0 ** (-(h + 1))


def _sigmoid(x):
    return 1.0 / (1.0 + jnp.exp(-x))


def _lane_iota(shape):
    return lax.broadcasted_iota(jnp.int32, shape, len(shape) - 1)


def _row_iota(shape):
    return lax.broadcasted_iota(jnp.int32, shape, 0)


def _group_norm(a, g, gmat):
    n = a.shape[1]
    msq = jnp.dot((a * a).astype(BF16), gmat[:n, :n], preferred_element_type=F32)
    return a * lax.rsqrt(msq + EPS) * g


def _place_pair(a_even, a_odd, k):
    lane = _lane_iota(a_even.shape)
    e_src = a_even if k == 0 else pltpu.roll(a_even, 64, axis=1)
    o_src = pltpu.roll(a_odd, 64, axis=1) if k == 0 else a_odd
    return jnp.where(lane < 64, e_src, o_src)


def _inproj_kernel(seq_len, tm, *refs):
    carried = seq_len >= tm
    if carried:
        (x_ref, gn_ref, w_ref, gq_ref, gks_ref, gkw_ref, gm_ref, cw_ref,
         q_out, kv_out, win_out, gate_out, sa_out, cp_out, u_out, carry) = refs
    else:
        (x_ref, gn_ref, w_ref, gq_ref, gks_ref, gkw_ref, gm_ref, cw_ref, i1_ref, i2_ref,
         q_out, kv_out, win_out, gate_out, sa_out, cp_out, u_out) = refs

    x = x_ref[0]
    ms = jnp.mean(x * x, axis=-1, keepdims=True)
    h = (x * lax.rsqrt(ms + EPS) * gn_ref[0]).astype(BF16)
    gmat = gm_ref[...]

    def mm(lo, hi):
        return jnp.dot(h, w_ref[0, :, lo:hi], preferred_element_type=F32)

    aq = mm(C_Q, C_KV)
    gq = gq_ref[0]
    for j in range(2):
        sl = slice(j * 256, (j + 1) * 256)
        qn = _group_norm(aq[:, sl], gq[:, sl], gmat) * (HEAD_DIM ** -0.5)
        q_out[0, :, sl] = qn.astype(BF16)

    akv = mm(C_KV, C_WIN)
    kv_out[0, :, 0:256] = akv[:, 0:256]
    kv_out[0, :, 256:384] = _group_norm(akv[:, 256:384], gks_ref[0], gmat)
    kv_out[0, :, 384:512] = akv[:, 384:512]

    aw = mm(C_WIN, C_ZA)
    win_out[0, :, 0:128] = _group_norm(aw[:, 0:128], gkw_ref[0], gmat)
    win_out[0, :, 128:256] = aw[:, 128:256]

    za = mm(C_ZA, C_BG)
    sa_out[0] = za * _sigmoid(za)

    gate_out[0] = _sigmoid(mm(C_GL, C_END))

    u = mm(C_CG, C_HV) * mm(C_HV, C_ZB)
    row = _row_iota(u.shape)
    r1 = pltpu.roll(u, 1, axis=0)
    r2 = pltpu.roll(u, 2, axis=0)
    if carried:
        @pl.when(pl.program_id(1) == 0)
        def _():
            carry[...] = jnp.zeros_like(carry)
        prev = carry[...]
        um1 = jnp.where(row == 0, prev[7:8], r1)
        um2 = jnp.where(row == 0, prev[6:7], jnp.where(row == 1, prev[7:8], r2))
        carry[...] = u[tm - 8:]
        u_out[0] = u[tm - 8:]
    else:
        pos = row % seq_len
        um1 = jnp.where(pos >= 1, r1, i1_ref[0])
        um2 = jnp.where(pos >= 2, r2, i2_ref[0])
        u_out[0] = u
    cw = cw_ref[0]
    y = cw[0:1] * um2 + cw[1:2] * um1 + cw[2:3] * u
    zb = mm(C_ZB, C_GL)
    cp_out[0] = (mm(C_BG, C_CG) * y * (zb * _sigmoid(zb))).astype(BF16)


def _inproj(layer, x, w_r, gn, gq_t, gks_t, gkw_t, gmat, conv_w, seq_len, tm, inits=None):
    nb, t = x.shape[:2]
    carried = seq_len >= tm
    grid = (nb, t // tm)
    tok = lambda n: pl.BlockSpec((1, tm, n), lambda b, s: (b, s, 0))
    lay2 = lambda n: pl.BlockSpec((1, 1, n), lambda b, s: (layer, 0, 0))
    in_specs = [
        tok(D_MODEL),
        lay2(D_MODEL),
        pl.BlockSpec((1, D_MODEL, C_END), lambda b, s: (layer, 0, 0)),
        lay2(512), lay2(128), lay2(128),
        pl.BlockSpec((256, 256), lambda b, s: (0, 0)),
        pl.BlockSpec((1, CONV_W, D_CONV), lambda b, s: (layer, 0, 0)),
    ]
    args = [x, gn, w_r, gq_t, gks_t, gkw_t, gmat, conv_w]
    scratch = []
    if carried:
        u_spec = pl.BlockSpec((1, 8, D_CONV), lambda b, s: (b, 0, 0))
        u_shape = (nb, 8, D_CONV)
        scratch = [pltpu.VMEM((8, D_CONV), F32)]
    else:
        in_specs += [tok(D_CONV), tok(D_CONV)]
        args += list(inits)
        u_spec = tok(D_CONV)
        u_shape = (nb, t, D_CONV)
    out_shape = [
        jax.ShapeDtypeStruct((nb, t, 512), BF16),
        jax.ShapeDtypeStruct((nb, t, 512), F32),
        jax.ShapeDtypeStruct((nb, t, 256), F32),
        jax.ShapeDtypeStruct((nb, t, 128), F32),
        jax.ShapeDtypeStruct((nb, t, 512), F32),
        jax.ShapeDtypeStruct((nb, t, 512), BF16),
        jax.ShapeDtypeStruct(u_shape, F32),
    ]
    out_specs = [tok(512), tok(512), tok(256), tok(128), tok(512), tok(512), u_spec]
    return pl.pallas_call(
        functools.partial(_inproj_kernel, seq_len, tm),
        grid=grid, in_specs=in_specs, out_specs=out_specs, out_shape=out_shape,
        scratch_shapes=scratch,
        compiler_params=pltpu.CompilerParams(
            dimension_semantics=("arbitrary", "arbitrary"), vmem_limit_bytes=VMEM_LIMIT),
        name="in_proj",
    )(*args)


def _compress_rows(load, n_half, wc_ref, pos_ref, gk_ref, gm_ref, out_ref):
    acc = jnp.zeros((2 * n_half, 256), F32)
    for l in range(CMP_BLOCK):
        xl = jnp.concatenate(
            [jnp.concatenate([load(half, l, n_half, 2 * CMP_BLOCK),
                              load(half, CMP_BLOCK + l, n_half, 2 * CMP_BLOCK)], axis=0)
             for half in range(2)], axis=1)
        xl = (xl + pos_ref[0, l:l + 1, :]).astype(BF16)
        acc = acc + jnp.dot(xl, wc_ref[0, l], preferred_element_type=F32)
    out_ref[0, :, 0:128] = _group_norm(acc[:, 0:128], gk_ref[0], gm_ref[...])
    out_ref[0, :, 128:256] = acc[:, 128:256]


def _compress_prompt_kernel(n_half, kc_ref, vc_ref, wc_ref, pos_ref, gk_ref, gm_ref, out_ref):
    refs = (kc_ref, vc_ref)
    load = lambda half, start, n, stride: refs[half][0, pl.ds(start, n, stride=stride), :]
    _compress_rows(load, n_half, wc_ref, pos_ref, gk_ref, gm_ref, out_ref)


def _compress_prompt(layer, kv, wc, posc, gk0_t, gmat):
    nb, s = kv.shape[:2]
    nc = s // CMP_BLOCK
    return pl.pallas_call(
        functools.partial(_compress_prompt_kernel, nc // 2),
        grid=(nb,),
        in_specs=[
            pl.BlockSpec((1, s, 128), lambda b: (b, 0, 0)),
            pl.BlockSpec((1, s, 128), lambda b: (b, 0, 1)),
            pl.BlockSpec((1, CMP_BLOCK, 256, 256), lambda b: (layer, 0, 0, 0)),
            pl.BlockSpec((1, CMP_BLOCK, 256), lambda b: (layer, 0, 0)),
            pl.BlockSpec((1, 1, 128), lambda b: (layer, 0, 0)),
            pl.BlockSpec((256, 256), lambda b: (0, 0)),
        ],
        out_specs=pl.BlockSpec((1, nc, 256), lambda b: (b, 0, 0)),
        out_shape=jax.ShapeDtypeStruct((nb, nc, 256), F32),
        compiler_params=pltpu.CompilerParams(
            dimension_semantics=("arbitrary",), vmem_limit_bytes=VMEM_LIMIT),
        name="compress_prompt",
    )(kv, kv, wc, posc, gk0_t, gmat)


def _topk_rows(score, blk, n):
    sel = jnp.zeros(score.shape, F32)
    for _ in range(n):
        mx = jnp.max(score, axis=0, keepdims=True)
        idx = jnp.min(jnp.where(score == mx, blk, 1 << 20), axis=0, keepdims=True)
        hit = blk == idx
        sel = jnp.where(hit, 1.0, sel)
        score = jnp.where(hit, -3e38, score)
    return sel


def _topk_lanes(score, blk, n):
    sel = jnp.zeros(score.shape, F32)
    for _ in range(n):
        mx = jnp.max(score, axis=1, keepdims=True)
        idx = jnp.min(jnp.where(score == mx, blk, 1 << 20), axis=1, keepdims=True)
        hit = blk == idx
        sel = jnp.where(hit, 1.0, sel)
        score = jnp.where(hit, -3e38, score)
    return sel


def _expand_gates(g, e_ref):
    g_hi = g.astype(BF16)
    g_lo = (g - g_hi.astype(F32)).astype(BF16)
    e = e_ref[...]
    return (jnp.dot(g_hi, e, preferred_element_type=F32)
            + jnp.dot(g_lo, e, preferred_element_type=F32))


def _attn_prompt_kernel(s_len, q_ref, kv_ref, win_ref, cmp_ref, gate_ref, sa_ref,
                        pc_ref, mk_ref, cc_ref, e_ref, o_ref,
                        ksa, vsb, kwa, vwb, kca, vcb, qa, m_scr, l_scr, acc_scr):
    qi = pl.program_id(1)
    q0 = qi * Q_TILE
    n_cmp = s_len // CMP_BLOCK
    n_blk = s_len // SEL_BLOCK

    @pl.when(qi == 0)
    def _build():
        def bld(c, carry):
            r = pl.ds(pl.multiple_of(c * 512, 512), 512)
            kvb = kv_ref[0, r, :]
            wb = win_ref[0, r, :]
            pc = pc_ref[r, :]
            lane = _lane_iota(pc.shape)
            for k in range(N_KV):
                ks = kvb[:, 0:128] if k == 0 else pltpu.roll(kvb[:, 0:128], 64, axis=1)
                kw = wb[:, 0:128] if k == 0 else pltpu.roll(wb[:, 0:128], 64, axis=1)
                ksa[k, r, 0:128] = jnp.where(lane < 64, ks, pc).astype(BF16)
                ksa[k, r, 128:256] = mk_ref[r, :]
                kwa[k, r, :] = jnp.where(lane < 64, kw, pc).astype(BF16)
            vsb[r, :] = kvb[:, 128:256].astype(BF16)
            vwb[r, :] = wb[:, 128:256].astype(BF16)
            return carry
        lax.fori_loop(0, s_len // 512, bld, 0)
        cm = cmp_ref[0]
        cc = cc_ref[...]
        lane = _lane_iota(cc.shape)
        for k in range(N_KV):
            kc = cm[:, 0:128] if k == 0 else pltpu.roll(cm[:, 0:128], 64, axis=1)
            kca[k] = jnp.where(lane < 64, kc, cc).astype(BF16)
        vcb[...] = cm[:, 128:256].astype(BF16)

    qf = q_ref[0].astype(F32)
    lane128 = _lane_iota((Q_TILE, LANES))
    trow = q0 + (_row_iota((GROUP * Q_TILE, 1)) % Q_TILE)

    o_tiles = [[None] * 4 for _ in range(3)]

    for k in range(N_KV):
        for h in range(GROUP):
            hg = k * GROUP + h
            tile = qf[:, (hg // 2) * 128:(hg // 2 + 1) * 128]
            src = tile if hg % 2 == 0 else pltpu.roll(tile, 64, axis=1)
            m = _slope(hg)
            bias = jnp.where(lane128 == 64, 64.0 * m,
                             jnp.where(lane128 == 65, m,
                                       jnp.where(lane128 == 66, -m * q0.astype(F32), 0.0)))
            qa[k, h * Q_TILE:(h + 1) * Q_TILE, 0:128] = jnp.where(lane128 < 64, src, bias).astype(BF16)

        st = lax.dot_general(kca[k], qa[k, :, 0:128], _NT, preferred_element_type=F32)
        r = _row_iota(st.shape)
        cend = CMP_BLOCK * (2 * (r % (n_cmp // 2)) + r // (n_cmp // 2)) + (CMP_BLOCK - 1)
        tq = q0 + (_lane_iota(st.shape) % Q_TILE)
        valid = cend <= tq
        st = jnp.where(valid, st, NEG)
        mx = jnp.max(st, axis=0, keepdims=True)
        e = jnp.where(valid, jnp.exp(st - mx), 0.0)
        pt = e / jnp.maximum(jnp.sum(e, axis=0, keepdims=True), 1e-30)

        imp = pt[:, 0:128] + pt[:, 128:256] + pt[:, 256:384] + pt[:, 384:512]
        imp = imp[0:n_blk] + imp[n_blk:2 * n_blk]
        blk = _row_iota(imp.shape)
        tcol = q0 + _lane_iota(imp.shape)
        cur = tcol // SEL_BLOCK
        forced = (blk == 0) | (blk == cur) | (blk == cur - 1)
        score = jnp.where(blk * SEL_BLOCK <= tcol,
                          imp + jnp.where(forced, FORCE_BONUS, 0.0), INVALID)
        sel = _topk_rows(score, blk, min(TOP_N, n_blk))
        notsel = jnp.concatenate([1.0 - sel, jnp.zeros((LANES - n_blk, Q_TILE), F32)], axis=0)
        notsel = notsel.T.astype(BF16)
        for h in range(GROUP):
            qa[k, h * Q_TILE:(h + 1) * Q_TILE, 128:256] = notsel

        oc = []
        for h in range(GROUP):
            p_h = pt[:, h * Q_TILE:(h + 1) * Q_TILE].T.astype(BF16)
            oc.append(jnp.dot(p_h, vcb[...], preferred_element_type=F32))

        m_scr[...] = jnp.full(m_scr.shape, -3e38, F32)
        l_scr[...] = jnp.zeros(l_scr.shape, F32)
        acc_scr[...] = jnp.zeros(acc_scr.shape, F32)

        def sel_body(c, carry):
            off = pl.multiple_of(c * SEL_CHUNK, SEL_CHUNK)
            kblk = ksa[k, pl.ds(off, SEL_CHUNK), :]
            s = lax.dot_general(qa[k], kblk, _NT, preferred_element_type=F32)
            kpos = off + _lane_iota(s.shape)
            s = jnp.where(kpos <= trow, s, NEG)
            m_old = m_scr[...]
            m_new = jnp.maximum(m_old, jnp.max(s, axis=1, keepdims=True))
            alpha = jnp.exp(m_old - m_new)
            p = jnp.exp(s - m_new)
            l_scr[...] = alpha * l_scr[...] + jnp.sum(p, axis=1, keepdims=True)
            acc_scr[...] = alpha * acc_scr[...] + jnp.dot(
                p.astype(BF16), vsb[pl.ds(off, SEL_CHUNK), :], preferred_element_type=F32)
            m_scr[...] = m_new
            return carry
        lax.fori_loop(0, (q0 + Q_TILE + SEL_CHUNK - 1) // SEL_CHUNK, sel_body, 0)
        osel = acc_scr[...] / l_scr[...]

        w_keys = WINDOW + Q_TILE
        start = pl.multiple_of(jnp.maximum(q0 - WINDOW, 0), Q_TILE)
        sw = lax.dot_general(qa[k, :, 0:128], kwa[k, pl.ds(start, w_keys), :], _NT,
                             preferred_element_type=F32)
        dist = trow - (start + _lane_iota(sw.shape))
        sw = jnp.where(dist.astype(jnp.uint32) < WINDOW, sw, NEG)
        mw = jnp.max(sw, axis=1, keepdims=True)
        pw = jnp.exp(sw - mw)
        lw = jnp.sum(pw, axis=1, keepdims=True)
        ow = jnp.dot(pw.astype(BF16), vwb[pl.ds(start, w_keys), :],
                     preferred_element_type=F32) / lw

        for pair in range(GROUP // 2):
            h0, h1 = 2 * pair, 2 * pair + 1
            rows = lambda a, h: a[h * Q_TILE:(h + 1) * Q_TILE]
            o_tiles[0][k * 2 + pair] = _place_pair(oc[h0], oc[h1], k)
            o_tiles[1][k * 2 + pair] = _place_pair(rows(osel, h0), rows(osel, h1), k)
            o_tiles[2][k * 2 + pair] = _place_pair(rows(ow, h0), rows(ow, h1), k)

    ge = _expand_gates(gate_ref[0], e_ref)
    sa = sa_ref[0]
    for t in range(4):
        sl = slice(t * 128, (t + 1) * 128)
        o = (ge[:, sl] * o_tiles[0][t] + ge[:, 512 + t * 128:512 + (t + 1) * 128] * o_tiles[1][t]
             + ge[:, 1024 + t * 128:1024 + (t + 1) * 128] * o_tiles[2][t])
        o_ref[0, :, sl] = (o * sa[:, sl]).astype(BF16)


def _attn_prompt(q, kv, win, cmpk, gates, sa, pc, mk, cc, emat):
    nb, s = q.shape[:2]
    nq = s // Q_TILE
    n_cmp = s // CMP_BLOCK
    tokq = lambda n: pl.BlockSpec((1, Q_TILE, n), lambda b, i: (b, i, 0))
    const = lambda a: pl.BlockSpec(a.shape, lambda b, i: (0,) * a.ndim)
    return pl.pallas_call(
        functools.partial(_attn_prompt_kernel, s),
        grid=(nb, nq),
        in_specs=[
            tokq(512),
            pl.BlockSpec((1, s, 256), lambda b, i: (b, 0, 1)),
            pl.BlockSpec((1, s, 256), lambda b, i: (b, 0, 0)),
            pl.BlockSpec((1, n_cmp, 256), lambda b, i: (b, 0, 0)),
            tokq(128), tokq(512),
            const(pc), const(mk), const(cc), const(emat),
        ],
        out_specs=tokq(512),
        out_shape=jax.ShapeDtypeStruct((nb, s, 512), BF16),
        scratch_shapes=[
            pltpu.VMEM((N_KV, s, 256), BF16),
            pltpu.VMEM((s, 128), BF16),
            pltpu.VMEM((N_KV, s, 128), BF16),
            pltpu.VMEM((s, 128), BF16),
            pltpu.VMEM((N_KV, n_cmp, 128), BF16),
            pltpu.VMEM((n_cmp, 128), BF16),
            pltpu.VMEM((N_KV, GROUP * Q_TILE, 256), BF16),
            pltpu.VMEM((GROUP * Q_TILE, 1), F32),
            pltpu.VMEM((GROUP * Q_TILE, 1), F32),
            pltpu.VMEM((GROUP * Q_TILE, 128), F32),
        ],
        compiler_params=pltpu.CompilerParams(
            dimension_semantics=("arbitrary", "arbitrary"), vmem_limit_bytes=VMEM_LIMIT),
        name="attn_prompt",
    )(q, kv, win, cmpk, gates, sa, pc, mk, cc, emat)


V_EXT = HEAD_DIM + 16


def _masked_exp(st, valid):
    st = jnp.where(valid, st, NEG)
    mx = jnp.max(st, axis=0, keepdims=True)
    return jnp.exp(st - mx), mx


def _normalize_ext(o_ext):
    return o_ext[0:HEAD_DIM] * (1.0 / o_ext[HEAD_DIM:HEAD_DIM + 1])


def _attn_prompt_t_kernel(s_len, q_ref, kv_ref, win_ref, cmp_ref, gate_ref, sa_ref,
                          pc_ref, mk_ref, cc_ref, o_ref,
                          ksa, vst, kwa, vwt, kca, vct, qa, acc_scr, m_scr, s_a, s_b):
    qi = pl.program_id(1)
    q0 = qi * Q_TILE
    n_cmp = s_len // CMP_BLOCK
    n_blk = s_len // SEL_BLOCK
    kt = SEL_CHUNK // LANES
    w_keys = WINDOW + Q_TILE
    wt = w_keys // LANES

    @pl.when(qi == 0)
    def _build():
        def bld(c, carry):
            r = pl.ds(pl.multiple_of(c * 512, 512), 512)
            kvb = kv_ref[0, r, :]
            wb = win_ref[0, r, :]
            pc = pc_ref[r, :]
            lane = _lane_iota(pc.shape)
            for k in range(N_KV):
                ks = kvb[:, 0:128] if k == 0 else pltpu.roll(kvb[:, 0:128], 64, axis=1)
                kw = wb[:, 0:128] if k == 0 else pltpu.roll(wb[:, 0:128], 64, axis=1)
                ksa[k, r, 0:128] = jnp.where(lane < 64, ks, pc).astype(BF16)
                ksa[k, r, 128:256] = mk_ref[r, :]
                kwa[k, r, :] = jnp.where(lane < 64, kw, pc).astype(BF16)
            ones = jnp.ones((V_EXT - HEAD_DIM, LANES), BF16)
            for j in range(512 // LANES):
                rows = slice(j * LANES, (j + 1) * LANES)
                for dst, src in ((vst, kvb), (vwt, wb)):
                    t = src[rows, 128:256].T.astype(BF16)
                    tile = jnp.concatenate([t[0:HEAD_DIM], ones, t[HEAD_DIM:], ones], axis=0)
                    dst[c * (512 // LANES) + j] = tile
            return carry
        lax.fori_loop(0, s_len // 512, bld, 0)
        cm = cmp_ref[0]
        cc = cc_ref[...]
        lane = _lane_iota(cc.shape)
        for k in range(N_KV):
            kc = cm[:, 0:128] if k == 0 else pltpu.roll(cm[:, 0:128], 64, axis=1)
            kca[k] = jnp.where(lane < 64, kc, cc).astype(BF16)
        vct[...] = cm[:, 128:256].T.astype(BF16)

    qf = q_ref[0].astype(F32)
    lane128 = _lane_iota((Q_TILE, LANES))
    hq = GROUP * Q_TILE
    tq = q0 + (_lane_iota((1, hq)) % Q_TILE)

    for k in range(N_KV):
        for h in range(GROUP):
            hg = k * GROUP + h
            tile = qf[:, (hg // 2) * 128:(hg // 2 + 1) * 128]
            src = tile if hg % 2 == 0 else pltpu.roll(tile, 64, axis=1)
            m = _slope(hg)
            bias = jnp.where(lane128 == 64, 64.0 * m,
                             jnp.where(lane128 == 65, m,
                                       jnp.where(lane128 == 66, -m * q0.astype(F32), 0.0)))
            qa[k, h * Q_TILE:(h + 1) * Q_TILE, 0:128] = jnp.where(lane128 < 64, src, bias).astype(BF16)

    oct_ = []
    for k in range(N_KV):
        dims = slice(k * HEAD_DIM, (k + 1) * HEAD_DIM)
        st = lax.dot_general(kca[k], qa[k, :, 0:128], _NT, preferred_element_type=F32)
        r = _row_iota(st.shape)
        cend = CMP_BLOCK * (2 * (r % (n_cmp // 2)) + r // (n_cmp // 2)) + (CMP_BLOCK - 1)
        e, mx = _masked_exp(st, cend <= tq)
        den = jnp.maximum(jnp.sum(e, axis=0, keepdims=True), 1e-30)
        pt = e * jnp.where(mx > 0.5 * NEG, 1.0 / den, 0.0)
        oct_.append(jnp.dot(vct[dims, :], pt.astype(BF16), preferred_element_type=F32))

        imp = pt[:, 0:128] + pt[:, 128:256] + pt[:, 256:384] + pt[:, 384:512]
        imp = imp[0:n_blk] + imp[n_blk:2 * n_blk]
        blk = _row_iota(imp.shape)
        tcol = q0 + _lane_iota(imp.shape)
        cur = tcol // SEL_BLOCK
        forced = (blk == 0) | (blk == cur) | (blk == cur - 1)
        score = jnp.where(blk * SEL_BLOCK <= tcol,
                          imp + jnp.where(forced, FORCE_BONUS, 0.0), INVALID)
        sel = _topk_rows(score, blk, min(TOP_N, n_blk))
        notsel = jnp.concatenate([1.0 - sel, jnp.zeros((LANES - n_blk, Q_TILE), F32)], axis=0)
        notsel = notsel.T.astype(BF16)
        for h in range(GROUP):
            qa[k, h * Q_TILE:(h + 1) * Q_TILE, 128:256] = notsel

    SEL, WIN = 0, 1

    def scores_into(dst, branch, off, nkeys):
        for k in range(N_KV):
            if branch == SEL:
                s = lax.dot_general(ksa[k, pl.ds(off, nkeys), :], qa[k], _NT,
                                    preferred_element_type=F32)
            else:
                s = lax.dot_general(kwa[k, pl.ds(off, nkeys), :], qa[k, :, 0:128], _NT,
                                    preferred_element_type=F32)
            dst[k, 0:nkeys] = s

    def consume(src, branch, off, nkeys, masked):
        vals = vst if branch == SEL else vwt
        for k in range(N_KV):
            s = src[k, 0:nkeys]
            if masked:
                dist = tq - (off + _row_iota(s.shape))
                ok = (dist >= 0) if branch == SEL else (dist.astype(jnp.uint32) < WINDOW)
                s = jnp.where(ok, s, NEG)
            m_old = m_scr[branch, k]
            m_new = jnp.maximum(m_old, jnp.max(s, axis=0, keepdims=True))
            alpha = jnp.exp(m_old - m_new)
            p = jnp.exp(s - m_new)
            v = vals[pl.ds(off // LANES, nkeys // LANES), k * V_EXT:(k + 1) * V_EXT, :]
            v = jnp.concatenate([v[j] for j in range(nkeys // LANES)], axis=1)
            acc_scr[branch, k] = alpha * acc_scr[branch, k] + jnp.dot(
                v, p.astype(BF16), preferred_element_type=F32)
            m_scr[branch, k] = m_new

    acc_scr[...] = jnp.zeros(acc_scr.shape, F32)
    m_scr[...] = jnp.full(m_scr.shape, -3e38, F32)

    sel_off = lambda c: pl.multiple_of(c * SEL_CHUNK, SEL_CHUNK)
    w0 = pl.multiple_of(jnp.maximum(q0 - WINDOW, 0), Q_TILE)
    w1 = pl.multiple_of(w0 + SEL_CHUNK, Q_TILE)
    w2 = pl.multiple_of(w0 + 2 * SEL_CHUNK, Q_TILE)
    w_last = w_keys - 2 * SEL_CHUNK
    n_full = q0 // SEL_CHUNK
    n_pairs = n_full // 2

    scores_into(s_a, WIN, w0, SEL_CHUNK)
    scores_into(s_b, WIN, w1, SEL_CHUNK)
    consume(s_a, WIN, w0, SEL_CHUNK, True)
    scores_into(s_a, WIN, w2, w_last)
    consume(s_b, WIN, w1, SEL_CHUNK, True)
    scores_into(s_b, SEL, sel_off(0), SEL_CHUNK)
    consume(s_a, WIN, w2, w_last, True)

    def body(i, carry):
        scores_into(s_a, SEL, sel_off(2 * i + 1), SEL_CHUNK)
        consume(s_b, SEL, sel_off(2 * i), SEL_CHUNK, False)
        scores_into(s_b, SEL, sel_off(2 * i + 2), SEL_CHUNK)
        consume(s_a, SEL, sel_off(2 * i + 1), SEL_CHUNK, False)
        return carry
    lax.fori_loop(0, n_pairs, body, 0)

    @pl.when(n_full % 2 == 0)
    def _():
        consume(s_b, SEL, sel_off(n_full), SEL_CHUNK, True)

    @pl.when(n_full % 2 == 1)
    def _():
        scores_into(s_a, SEL, sel_off(n_full), SEL_CHUNK)
        consume(s_b, SEL, sel_off(n_full - 1), SEL_CHUNK, False)
        consume(s_a, SEL, sel_off(n_full), SEL_CHUNK, True)

    gt = gate_ref[0].T
    sa = sa_ref[0]
    for k in range(N_KV):
        ost = _normalize_ext(acc_scr[SEL, k])
        owk = _normalize_ext(acc_scr[WIN, k])
        for pair in range(GROUP // 2):
            halves = []
            for h in (2 * pair, 2 * pair + 1):
                hg = k * GROUP + h
                cols = slice(h * Q_TILE, (h + 1) * Q_TILE)
                halves.append(gt[3 * hg:3 * hg + 1] * oct_[k][:, cols]
                              + gt[3 * hg + 1:3 * hg + 2] * ost[:, cols]
                              + gt[3 * hg + 2:3 * hg + 3] * owk[:, cols])
            t = k * 2 + pair
            o = jnp.concatenate(halves, axis=0).T
            o_ref[0, :, t * 128:(t + 1) * 128] = (o * sa[:, t * 128:(t + 1) * 128]).astype(BF16)


def _attn_prompt_t(q, kv, win, cmpk, gates, sa, pc, mk, cc):
    nb, s = q.shape[:2]
    nq = s // Q_TILE
    n_cmp = s // CMP_BLOCK
    tokq = lambda n: pl.BlockSpec((1, Q_TILE, n), lambda b, i: (b, i, 0))
    const = lambda a: pl.BlockSpec(a.shape, lambda b, i: (0,) * a.ndim)
    return pl.pallas_call(
        functools.partial(_attn_prompt_t_kernel, s),
        grid=(nb, nq),
        in_specs=[
            tokq(512),
            pl.BlockSpec((1, s, 256), lambda b, i: (b, 0, 1)),
            pl.BlockSpec((1, s, 256), lambda b, i: (b, 0, 0)),
            pl.BlockSpec((1, n_cmp, 256), lambda b, i: (b, 0, 0)),
            tokq(128), tokq(512),
            const(pc), const(mk), const(cc),
        ],
        out_specs=tokq(512),
        out_shape=jax.ShapeDtypeStruct((nb, s, 512), BF16),
        scratch_shapes=[
            pltpu.VMEM((N_KV, s, 256), BF16),
            pltpu.VMEM((s // LANES, N_KV * V_EXT, LANES), BF16),
            pltpu.VMEM((N_KV, s, 128), BF16),
            pltpu.VMEM((s // LANES, N_KV * V_EXT, LANES), BF16),
            pltpu.VMEM((N_KV, n_cmp, 128), BF16),
            pltpu.VMEM((128, n_cmp), BF16),
            pltpu.VMEM((N_KV, GROUP * Q_TILE, 256), BF16),
            pltpu.VMEM((2, N_KV, V_EXT, GROUP * Q_TILE), F32),
            pltpu.VMEM((2, N_KV, 1, GROUP * Q_TILE), F32),
            pltpu.VMEM((N_KV, SEL_CHUNK, GROUP * Q_TILE), F32),
            pltpu.VMEM((N_KV, SEL_CHUNK, GROUP * Q_TILE), F32),
        ],
        compiler_params=pltpu.CompilerParams(
            dimension_semantics=("arbitrary", "arbitrary"), vmem_limit_bytes=VMEM_LIMIT),
        name="attn_prompt",
    )(q, kv, win, cmpk, gates, sa, pc, mk, cc)


def _outproj_kernel(x_ref, oa_ref, cp_ref, p_ref, wo_ref, wple_ref, wpg_ref, g_ref, y_ref):
    x1 = (x_ref[0]
          + jnp.dot(oa_ref[0], wo_ref[0, 0:D_ATT, :], preferred_element_type=F32)
          + jnp.dot(cp_ref[0], wo_ref[0, D_ATT:, :], preferred_element_type=F32))
    ms = jnp.mean(x1 * x1, axis=-1, keepdims=True)
    h2 = (x1 * lax.rsqrt(ms + EPS) * g_ref[0]).astype(BF16)
    gate = _sigmoid(jnp.dot(h2, wpg_ref[0], preferred_element_type=F32))
    pe = jnp.dot(p_ref[0, 0].astype(BF16), wple_ref[0], preferred_element_type=F32)
    y_ref[0] = x1 + pe * gate


def _outproj(layer, x, oa, cp, p_all, wo, wple, wpg, g_ple, tm):
    nb, t = x.shape[:2]
    tok = lambda n: pl.BlockSpec((1, tm, n), lambda b, s: (b, s, 0))
    return pl.pallas_call(
        _outproj_kernel,
        grid=(nb, t // tm),
        in_specs=[
            tok(D_MODEL), tok(512), tok(512),
            pl.BlockSpec((1, 1, tm, D_PLE), lambda b, s: (layer, b, s, 0)),
            pl.BlockSpec((1, D_MODEL, D_MODEL), lambda b, s: (layer, 0, 0)),
            pl.BlockSpec((1, D_PLE, D_MODEL), lambda b, s: (layer, 0, 0)),
            pl.BlockSpec((1, D_MODEL, D_MODEL), lambda b, s: (layer, 0, 0)),
            pl.BlockSpec((1, 1, D_MODEL), lambda b, s: (layer, 0, 0)),
        ],
        out_specs=tok(D_MODEL),
        out_shape=jax.ShapeDtypeStruct((nb, t, D_MODEL), F32),
        compiler_params=pltpu.CompilerParams(
            dimension_semantics=("arbitrary", "arbitrary"), vmem_limit_bytes=VMEM_LIMIT),
        name="out_proj",
    )(x, oa, cp, p_all, wo, wple, wpg, g_ple)


def _page_copy(cache_ref, pt_ref, buf, sem, b, slot, n_pages, row0):
    return [pltpu.make_async_copy(
        cache_ref.at[pt_ref[b, p], pl.ds(row0, 256), :],
        buf.at[slot, pl.ds(p * 256, 256), :],
        sem.at[slot]) for p in range(n_pages)]


def _paged_prefetch(cache_ref, pt_ref, buf, sem, n_pages, row0):
    b = pl.program_id(0)
    slot = b % 2

    @pl.when(b == 0)
    def _():
        for cp in _page_copy(cache_ref, pt_ref, buf, sem, 0, 0, n_pages, row0):
            cp.start()

    @pl.when(b + 1 < pl.num_programs(0))
    def _():
        for cp in _page_copy(cache_ref, pt_ref, buf, sem, b + 1, 1 - slot, n_pages, row0):
            cp.start()

    for cp in _page_copy(cache_ref, pt_ref, buf, sem, b, slot, n_pages, row0):
        cp.wait()
    return slot


def _compress_sample_kernel(n_pages, pt_ref, cache_ref, wc_ref, post_ref, perm_ref, gk_ref, gm_ref,
                            out_ref, buf, sem, zbuf):
    slot = _paged_prefetch(cache_ref, pt_ref, buf, sem, n_pages, 0)
    per_pair = 2 * PAGE_SIZE // CMP_BLOCK

    def regroup(g, carry):
        xt = jnp.concatenate(
            [buf[slot, pl.ds(pl.multiple_of((2 * g + j) * 256, 256), 256), :] for j in range(2)],
            axis=1)
        xt = (xt + post_ref[0]).astype(BF16)
        z = lax.dot_general(perm_ref[...], xt, _NT, preferred_element_type=F32)
        rows = pl.ds(pl.multiple_of(g * per_pair, per_pair), per_pair)
        for l in range(CMP_BLOCK):
            zbuf[l, rows, :] = z[l * per_pair:(l + 1) * per_pair]

    def regroup4(g4, carry):
        for j in range(4):
            regroup(4 * g4 + j, carry)
        return carry
    lax.fori_loop(0, n_pages // 8, regroup4, 0)

    acc = jnp.zeros((zbuf.shape[1], 256), F32)
    for l in range(CMP_BLOCK):
        acc = acc + jnp.dot(zbuf[l].astype(BF16), wc_ref[0, l], preferred_element_type=F32)
    out_ref[0, :, 0:128] = _group_norm(acc[:, 0:128], gk_ref[0], gm_ref[...])
    out_ref[0, :, 128:256] = acc[:, 128:256]


def _compress_sample(layer, pt, cache3, wc, post, perm, gk0_t, gmat):
    nb, n_pages = pt.shape
    nc = n_pages * PAGE_SIZE // CMP_BLOCK
    grid_spec = pltpu.PrefetchScalarGridSpec(
        num_scalar_prefetch=1,
        grid=(nb,),
        in_specs=[
            pl.BlockSpec(memory_space=pl.ANY),
            pl.BlockSpec((1, CMP_BLOCK, 256, 256), lambda b, pt: (layer, 0, 0, 0)),
            pl.BlockSpec((1, 256, 2 * PAGE_SIZE), lambda b, pt: (layer, 0, 0)),
            pl.BlockSpec((2 * PAGE_SIZE, 2 * PAGE_SIZE), lambda b, pt: (0, 0)),
            pl.BlockSpec((1, 1, 128), lambda b, pt: (layer, 0, 0)),
            pl.BlockSpec((256, 256), lambda b, pt: (0, 0)),
        ],
        out_specs=pl.BlockSpec((1, nc, 256), lambda b, pt: (b, 0, 0)),
        scratch_shapes=[
            pltpu.VMEM((2, n_pages * 256, PAGE_SIZE), F32),
            pltpu.SemaphoreType.DMA((2,)),
            pltpu.VMEM((CMP_BLOCK, nc, 256), F32),
        ],
    )
    return pl.pallas_call(
        functools.partial(_compress_sample_kernel, n_pages),
        grid_spec=grid_spec,
        out_shape=jax.ShapeDtypeStruct((nb, nc, 256), F32),
        compiler_params=pltpu.CompilerParams(
            dimension_semantics=("arbitrary",), vmem_limit_bytes=VMEM_LIMIT),
        name="compress_sample",
    )(pt, cache3, wc, post, perm, gk0_t, gmat)


def _attn_sample_kernel(n_pages, dec_len, pt_ref, cache_ref, q_ref, kvn_ref, wn_ref,
                        cwin_ref, cmp_ref, gate_ref, sa_ref, mk_ref, e_ref, pair_ref, o_ref,
                        buf, sem, kta, vtb):
    past = n_pages * PAGE_SIZE
    n_cmp = past // CMP_BLOCK
    n_blk = past // SEL_BLOCK
    rows = N_KV * GROUP * dec_len
    hrows = GROUP * dec_len

    @pl.when(pl.program_id(0) == 0)
    def _():
        kta[128:256, :] = mk_ref[...]

    slot = _paged_prefetch(cache_ref, pt_ref, buf, sem, n_pages, 256)
    for p in range(n_pages):
        blk = buf[slot, p * 256:(p + 1) * 256, :]
        kta[0:128, p * PAGE_SIZE:(p + 1) * PAGE_SIZE] = blk[0:128].astype(BF16)
        vtb[:, p * PAGE_SIZE:(p + 1) * PAGE_SIZE] = blk[128:256].astype(BF16)

    qf = q_ref[0].astype(F32)
    lane = _lane_iota((dec_len, LANES))
    pieces = []
    for k in range(N_KV):
        for h in range(GROUP):
            hg = k * GROUP + h
            tile = qf[:, (hg // 2) * 128:(hg // 2 + 1) * 128]
            src = tile if (hg % 2) == k else pltpu.roll(tile, 64, axis=1)
            keep = (lane < 64) if k == 0 else (lane >= 64)
            pieces.append(jnp.where(keep, src, 0.0))
    qbd = jnp.concatenate(pieces, axis=0)
    qbd16 = qbd.astype(BF16)

    rid = _row_iota((rows, 1))
    slope = jnp.zeros((rows, 1), F32)
    for hg in range(N_HEADS):
        slope = jnp.where(rid // dec_len == hg, _slope(hg), slope)
    qpos = rid % dec_len

    cm = cmp_ref[0]
    sc = lax.dot_general(qbd16, cm[:, 0:128].astype(BF16), _NT, preferred_element_type=F32)
    cend = CMP_BLOCK * _lane_iota(sc.shape) + (CMP_BLOCK - 1)
    sc = sc + slope * (cend - past).astype(F32)
    mc = jnp.max(sc, axis=1, keepdims=True)
    ec = jnp.exp(sc - mc)
    pcm = ec / jnp.maximum(jnp.sum(ec, axis=1, keepdims=True), 1e-30)
    ocm = jnp.dot(pcm.astype(BF16), cm[:, 128:256].astype(BF16), preferred_element_type=F32)

    scores = []
    for k in range(N_KV):
        imp = pcm[k * hrows:k * hrows + dec_len]
        for h in range(1, GROUP):
            imp = imp + pcm[k * hrows + h * dec_len:k * hrows + (h + 1) * dec_len]
        hi = imp.astype(BF16)
        mid = (imp - hi.astype(F32)).astype(BF16)
        lo = (imp - hi.astype(F32) - mid.astype(F32)).astype(BF16)
        imp = sum(jnp.dot(part, pair_ref[...], preferred_element_type=F32)
                  for part in (hi, mid, lo))
        blk = _lane_iota(imp.shape)
        t = past + _row_iota(imp.shape)
        cur = t // SEL_BLOCK
        forced = (blk == 0) | (blk == cur) | (blk == cur - 1)
        score = imp + jnp.where(forced, FORCE_BONUS, 0.0)
        if n_blk < LANES:
            score = jnp.concatenate([score, jnp.full((dec_len, LANES - n_blk), -3e38, F32)], axis=1)
        scores.append(score)
    score_t = jnp.concatenate(
        scores + [jnp.zeros((LANES - N_KV * dec_len, LANES), F32)], axis=0).T
    sel_t = _topk_rows(score_t, _row_iota(score_t.shape), TOP_N - 1)
    ns = 1.0 - sel_t.T
    notsel = jnp.concatenate(
        [ns[k * dec_len:(k + 1) * dec_len] for k in range(N_KV) for _ in range(GROUP)], axis=0)
    qaug = jnp.concatenate([qbd16, notsel.astype(BF16)], axis=1)

    s1 = jnp.dot(qaug, kta[...], preferred_element_type=F32)
    s1 = s1 + slope * (_lane_iota(s1.shape) - past).astype(F32)
    kn = jnp.concatenate([kvn_ref[0][:, 256:384], jnp.zeros((LANES - dec_len, 128), F32)], axis=0)
    vn = jnp.concatenate([kvn_ref[0][:, 384:512], jnp.zeros((LANES - dec_len, 128), F32)], axis=0)
    s2 = lax.dot_general(qbd16, kn.astype(BF16), _NT, preferred_element_type=F32)
    j2 = _lane_iota(s2.shape)
    s2 = jnp.where(j2 <= qpos, s2 + slope * j2.astype(F32), NEG)
    ms = jnp.maximum(jnp.max(s1, axis=1, keepdims=True), jnp.max(s2, axis=1, keepdims=True))
    p1 = jnp.exp(s1 - ms)
    p2 = jnp.exp(s2 - ms)
    ls = jnp.sum(p1, axis=1, keepdims=True) + jnp.sum(p2, axis=1, keepdims=True)
    osl = (lax.dot_general(p1.astype(BF16), vtb[...], _NT, preferred_element_type=F32)
           + jnp.dot(p2.astype(BF16), vn.astype(BF16), preferred_element_type=F32)) / ls

    cw = cwin_ref[0, 0]
    wb = cw.shape[1]
    w1 = jnp.dot(qbd16, cw[0:128].astype(BF16), preferred_element_type=F32)
    jw = _lane_iota(w1.shape)
    dist = qpos + (wb - jw)
    w1 = jnp.where(dist < WINDOW, w1 + slope * (jw - wb).astype(F32), NEG)
    kwn = jnp.concatenate([wn_ref[0][:, 0:128], jnp.zeros((LANES - dec_len, 128), F32)], axis=0)
    vwn = jnp.concatenate([wn_ref[0][:, 128:256], jnp.zeros((LANES - dec_len, 128), F32)], axis=0)
    w2 = lax.dot_general(qbd16, kwn.astype(BF16), _NT, preferred_element_type=F32)
    w2 = jnp.where(j2 <= qpos, w2 + slope * j2.astype(F32), NEG)
    mw = jnp.maximum(jnp.max(w1, axis=1, keepdims=True), jnp.max(w2, axis=1, keepdims=True))
    pw1 = jnp.exp(w1 - mw)
    pw2 = jnp.exp(w2 - mw)
    lw = jnp.sum(pw1, axis=1, keepdims=True) + jnp.sum(pw2, axis=1, keepdims=True)
    owd = (lax.dot_general(pw1.astype(BF16), cw[128:256].astype(BF16), _NT,
                           preferred_element_type=F32)
           + jnp.dot(pw2.astype(BF16), vwn.astype(BF16), preferred_element_type=F32)) / lw

    ge = _expand_gates(gate_ref[0], e_ref)
    sa = sa_ref[0]
    for t4 in range(4):
        k, pair = t4 // 2, t4 % 2
        r0 = (k * GROUP + 2 * pair) * dec_len
        r1 = r0 + dec_len
        rows_of = lambda a, r: a[r:r + dec_len]
        sl = slice(t4 * 128, (t4 + 1) * 128)
        o = (ge[:, sl] * _place_pair(rows_of(ocm, r0), rows_of(ocm, r1), k)
             + ge[:, 512 + t4 * 128:512 + (t4 + 1) * 128]
             * _place_pair(rows_of(osl, r0), rows_of(osl, r1), k)
             + ge[:, 1024 + t4 * 128:1024 + (t4 + 1) * 128]
             * _place_pair(rows_of(owd, r0), rows_of(owd, r1), k))
        o_ref[0, :, sl] = (o * sa[:, sl]).astype(BF16)


def _attn_sample(layer, pt, cache3, q, kvn, wn, cwin, cmpk, gates, sa, mk, emat, pairm):
    nb, n_pages = pt.shape
    dec_len = q.shape[1]
    past = n_pages * PAGE_SIZE
    wb = cwin.shape[3]
    tok = lambda n: pl.BlockSpec((1, dec_len, n), lambda b, pt: (b, 0, 0))
    grid_spec = pltpu.PrefetchScalarGridSpec(
        num_scalar_prefetch=1,
        grid=(nb,),
        in_specs=[
            pl.BlockSpec(memory_space=pl.ANY),
            tok(512), tok(512), tok(256),
            pl.BlockSpec((1, 1, 256, wb), lambda b, pt: (layer, b, 0, 0)),
            pl.BlockSpec((1, past // CMP_BLOCK, 256), lambda b, pt: (b, 0, 0)),
            tok(128), tok(512),
            pl.BlockSpec(mk.shape, lambda b, pt: (0, 0)),
            pl.BlockSpec(emat.shape, lambda b, pt: (0, 0)),
            pl.BlockSpec(pairm.shape, lambda b, pt: (0, 0)),
        ],
        out_specs=tok(512),
        scratch_shapes=[
            pltpu.VMEM((2, n_pages * 256, PAGE_SIZE), F32),
            pltpu.SemaphoreType.DMA((2,)),
            pltpu.VMEM((256, past), BF16),
            pltpu.VMEM((128, past), BF16),
        ],
    )
    return pl.pallas_call(
        functools.partial(_attn_sample_kernel, n_pages, dec_len),
        grid_spec=grid_spec,
        out_shape=jax.ShapeDtypeStruct((nb, dec_len, 512), BF16),
        compiler_params=pltpu.CompilerParams(
            dimension_semantics=("arbitrary",), vmem_limit_bytes=VMEM_LIMIT),
        name="attn_sample",
    )(pt, cache3, q, kvn, wn, cwin, cmpk, gates, sa, mk, emat, pairm)


def _key_consts(n_keys, n_cmp):
    kpos = np.arange(n_keys)
    pc = np.zeros((n_keys, LANES), np.float32)
    pc[:, 64] = kpos // 64
    pc[:, 65] = kpos % 64
    pc[:, 66] = 1.0
    mk = np.zeros((n_keys, LANES), np.float32)
    mk[kpos, (kpos // SEL_BLOCK) % LANES] = -BIG
    r = np.arange(n_cmp)
    cend = CMP_BLOCK * (2 * (r % (n_cmp // 2)) + r // (n_cmp // 2)) + CMP_BLOCK - 1
    cc = np.zeros((n_cmp, LANES), np.float32)
    cc[:, 64] = cend // 64
    cc[:, 65] = cend % 64
    cc[:, 66] = 1.0
    return jnp.asarray(pc), jnp.asarray(mk, dtype=BF16), jnp.asarray(cc)


def _block_mask_rows(n_keys):
    kpos = np.arange(n_keys)
    mk = np.zeros((LANES, n_keys), np.float32)
    mk[(kpos // SEL_BLOCK) % LANES, kpos] = -BIG
    return jnp.asarray(mk, dtype=BF16)


def _gate_expand_matrix():
    e = np.zeros((LANES, 3 * D_ATT), np.float32)
    for h in range(N_HEADS):
        for j in range(3):
            e[h * 3 + j, j * D_ATT + h * HEAD_DIM:j * D_ATT + (h + 1) * HEAD_DIM] = 1.0
    return jnp.asarray(e, dtype=BF16)


def _group_mean_matrix():
    i = np.arange(256)
    return jnp.asarray((i[:, None] // HEAD_DIM == i[None, :] // HEAD_DIM) / HEAD_DIM, dtype=BF16)


def _reorder_w_in(w_in):
    o = np.cumsum([0, 512, 128, 128, 128, 128, 128, 128, 24, 512, 512, 512, 512, 512])
    q, kc, vc, ks, vs, kw, vw, gl, za, bg, cg, hv, zb = [
        w_in[..., o[i]:o[i + 1]] for i in range(13)]
    pad = jnp.zeros(w_in.shape[:-1] + (LANES - 24,), w_in.dtype)
    return jnp.concatenate([q, kc, vc, ks, vs, kw, vw, za, bg, cg, hv, zb, gl, pad],
                           axis=-1).astype(BF16)


def _compress_weights(w_phi, cmp_pos):
    w16 = w_phi.astype(BF16)
    zero = jnp.zeros_like(w16[:, 0])
    blocks = [w16[:, 0], w16[:, 0], w16[:, 1], w16[:, 1]]
    wc = jnp.concatenate(
        [jnp.concatenate([blocks[i] if i == j else zero for j in range(4)], axis=-1)
         for i in range(4)], axis=-2)
    posc = jnp.concatenate([cmp_pos[:, 0], cmp_pos[:, 0], cmp_pos[:, 1], cmp_pos[:, 1]], axis=-1)
    post = jnp.tile(jnp.transpose(posc, (0, 2, 1)), (1, 1, 2 * PAGE_SIZE // CMP_BLOCK))
    return wc, posc, post


def _regroup_matrix():
    n = 2 * PAGE_SIZE
    per_pair = n // CMP_BLOCK
    p = np.zeros((n, n), np.float32)
    for l in range(CMP_BLOCK):
        for c in range(per_pair):
            p[l * per_pair + c, CMP_BLOCK * c + l] = 1.0
    return jnp.asarray(p, dtype=BF16)


def _pair_sum_matrix(n_cmp):
    p = np.zeros((n_cmp, n_cmp // 2), np.float32)
    p[np.arange(n_cmp), np.arange(n_cmp) // 2] = 1.0
    return jnp.asarray(p, dtype=BF16)


def kernel(x_prompt, x_sample, cache_kv, cache_win, state_conv, page_table, p_prompt, p_sample,
           g_norm, w_in, g_q, g_k, cmp_pos, w_phi, conv_w, w_out, w_ple, w_pg, g_ple):
    depth = w_in.shape[0]
    nb, s_len = x_prompt.shape[:2]
    db, dec_len = x_sample.shape[:2]
    n_pool = cache_kv.shape[1]
    n_pages = page_table.shape[1]
    past = n_pages * PAGE_SIZE
    wb = cache_win.shape[2]

    w_r = _reorder_w_in(w_in)
    wo16, wple16, wpg16 = w_out.astype(BF16), w_ple.astype(BF16), w_pg.astype(BF16)
    wc, posc, post = _compress_weights(w_phi, cmp_pos)
    perm = _regroup_matrix()
    pairm = _pair_sum_matrix(past // CMP_BLOCK)
    gq_t = jnp.tile(g_q, (1, N_HEADS))[:, None]
    gk0_t = jnp.tile(g_k[:, 0], (1, N_KV))[:, None]
    gks_t = jnp.tile(g_k[:, 1], (1, N_KV))[:, None]
    gkw_t = jnp.tile(g_k[:, 2], (1, N_KV))[:, None]
    g_norm = g_norm[:, None]
    g_ple = g_ple[:, None]
    gmat = _group_mean_matrix()
    emat = _gate_expand_matrix()
    pc_p, mk_p, cc_p = _key_consts(s_len, s_len // CMP_BLOCK)
    mk_s = _block_mask_rows(past)

    cache3 = jnp.transpose(cache_kv, (0, 1, 3, 4, 5, 2)).reshape(
        depth * n_pool, 4 * N_KV * HEAD_DIM, PAGE_SIZE)
    cwin4 = jnp.transpose(cache_win, (0, 1, 3, 4, 5, 2)).reshape(
        depth, db, 2 * N_KV * HEAD_DIM, wb)
    xs = x_sample.reshape(1, db * dec_len, D_MODEL)
    ps = p_sample.reshape(depth, 1, db * dec_len, D_PLE)

    xp = x_prompt
    outs = [[] for _ in range(6)]
    for i in range(depth):
        q, kv, win, gates, sa, cp, tail = _inproj(
            i, xp, w_r, g_norm, gq_t, gks_t, gkw_t, gmat, conv_w, s_len, 512)
        cmpk = _compress_prompt(i, kv, wc, posc, gk0_t, gmat)
        oa = _attn_prompt_t(q, kv, win, cmpk, gates, sa, pc_p, mk_p, cc_p)
        xp = _outproj(i, xp, oa, cp, p_prompt, wo16, wple16, wpg16, g_ple, 512)
        outs[0].append(kv.reshape(nb, s_len, 4, N_KV, HEAD_DIM))
        outs[1].append(win[:, s_len - min(WINDOW, s_len):].reshape(nb, -1, 2, N_KV, HEAD_DIM))
        outs[2].append(tail[:, 8 - (CONV_W - 1):])

        st = state_conv[i]
        pad = jnp.zeros((db, dec_len - 1, D_CONV), F32)
        init1 = jnp.concatenate([st[:, 1:2], pad], axis=1).reshape(1, db * dec_len, D_CONV)
        init2 = jnp.concatenate([st[:, 0:1], st[:, 1:2], pad[:, 1:]], axis=1).reshape(
            1, db * dec_len, D_CONV)
        q, kv, win, gates, sa, cp, u = _inproj(
            i, xs, w_r, g_norm, gq_t, gks_t, gkw_t, gmat, conv_w, dec_len, db * dec_len,
            inits=(init1, init2))
        pt = page_table + i * n_pool
        cmpk = _compress_sample(i, pt, cache3, wc, post, perm, gk0_t, gmat)
        r3 = lambda a: a.reshape(db, dec_len, a.shape[-1])
        oa = _attn_sample(i, pt, cache3, r3(q), r3(kv), r3(win), cwin4, cmpk, r3(gates), r3(sa),
                          mk_s, emat, pairm)
        xs = _outproj(i, xs, oa.reshape(1, db * dec_len, 512), cp, ps, wo16, wple16, wpg16,
                      g_ple, db * dec_len)
        outs[3].append(kv.reshape(db, dec_len, 4, N_KV, HEAD_DIM))
        new_win = jnp.concatenate(
            [cwin4[i][:, :, dec_len:], jnp.transpose(r3(win), (0, 2, 1))], axis=2)
        outs[4].append(jnp.transpose(new_win.reshape(db, 2, N_KV, HEAD_DIM, wb), (0, 4, 1, 2, 3)))
        outs[5].append(u.reshape(db, dec_len, D_CONV)[:, dec_len - (CONV_W - 1):])

    return (xp, xs.reshape(db, dec_len, D_MODEL)) + tuple(jnp.stack(o) for o in outs)
```

```python
import functools

import numpy as np
import jax
import jax.numpy as jnp
from jax import lax
from jax.experimental import pallas as pl
from jax.experimental.pallas import tpu as pltpu

D_MODEL = 1024
N_HEADS = 8
HEAD_DIM = 64
N_KV = 2
GROUP = 4
D_ATT = 512
D_CONV = 512
D_KV = 128
CMP_BLOCK = 32
SEL_BLOCK = 64
TOP_N = 16
WINDOW = 512
CONV_W = 3
D_PLE = 256
PAGE_SIZE = 128
EPS = 1e-6
NEG = -1e30
INVALID = -1e9
FORCE_BONUS = 1e3
BIG = 1e30

LANES = 128
Q_TILE = 128
SEL_CHUNK = 256
VMEM_LIMIT = 56 * 1024 * 1024

F32 = jnp.float32
BF16 = jnp.bfloat16

C_Q, C_KV, C_WIN, C_ZA, C_BG, C_CG, C_HV, C_ZB, C_GL, C_END = (
    0, 512, 1024, 1280, 1792, 2304, 2816, 3328, 3840, 3968)

_NT = (((1,), (1,)), ((), ()))


def _slope(h):
    return 2.0 ** (-(h + 1))


def _sigmoid(x):
    return 1.0 / (1.0 + jnp.exp(-x))


def _lane_iota(shape):
    return lax.broadcasted_iota(jnp.int32, shape, len(shape) - 1)


def _row_iota(shape):
    return lax.broadcasted_iota(jnp.int32, shape, 0)


def _group_norm(a, g, gmat):
    n = a.shape[1]
    msq = jnp.dot((a * a).astype(BF16), gmat[:n, :n], preferred_element_type=F32)
    return a * lax.rsqrt(msq + EPS) * g


def _place_pair(a_even, a_odd, k):
    lane = _lane_iota(a_even.shape)
    e_src = a_even if k == 0 else pltpu.roll(a_even, 64, axis=1)
    o_src = pltpu.roll(a_odd, 64, axis=1) if k == 0 else a_odd
    return jnp.where(lane < 64, e_src, o_src)


def _inproj_kernel(seq_len, tm, *refs):
    carried = seq_len >= tm
    if carried:
        (x_ref, gn_ref, w_ref, gq_ref, gks_ref, gkw_ref, gm_ref, cw_ref,
         q_out, kv_out, win_out, gate_out, sa_out, cp_out, u_out, carry) = refs
    else:
        (x_ref, gn_ref, w_ref, gq_ref, gks_ref, gkw_ref, gm_ref, cw_ref, i1_ref, i2_ref,
         q_out, kv_out, win_out, gate_out, sa_out, cp_out, u_out) = refs

    x = x_ref[0]
    ms = jnp.mean(x * x, axis=-1, keepdims=True)
    h = (x * lax.rsqrt(ms + EPS) * gn_ref[0]).astype(BF16)
    gmat = gm_ref[...]

    def mm(lo, hi):
        return jnp.dot(h, w_ref[0, :, lo:hi], preferred_element_type=F32)

    aq = mm(C_Q, C_KV)
    gq = gq_ref[0]
    for j in range(2):
        sl = slice(j * 256, (j + 1) * 256)
        qn = _group_norm(aq[:, sl], gq[:, sl], gmat) * (HEAD_DIM ** -0.5)
        q_out[0, :, sl] = qn.astype(BF16)

    akv = mm(C_KV, C_WIN)
    kv_out[0, :, 0:256] = akv[:, 0:256]
    kv_out[0, :, 256:384] = _group_norm(akv[:, 256:384], gks_ref[0], gmat)
    kv_out[0, :, 384:512] = akv[:, 384:512]

    aw = mm(C_WIN, C_ZA)
    win_out[0, :, 0:128] = _group_norm(aw[:, 0:128], gkw_ref[0], gmat)
    win_out[0, :, 128:256] = aw[:, 128:256]

    za = mm(C_ZA, C_BG)
    sa_out[0] = za * _sigmoid(za)

    gate_out[0] = _sigmoid(mm(C_GL, C_END))

    u = mm(C_CG, C_HV) * mm(C_HV, C_ZB)
    row = _row_iota(u.shape)
    r1 = pltpu.roll(u, 1, axis=0)
    r2 = pltpu.roll(u, 2, axis=0)
    if carried:
        @pl.when(pl.program_id(1) == 0)
        def _():
            carry[...] = jnp.zeros_like(carry)
        prev = carry[...]
        um1 = jnp.where(row == 0, prev[7:8], r1)
        um2 = jnp.where(row == 0, prev[6:7], jnp.where(row == 1, prev[7:8], r2))
        carry[...] = u[tm - 8:]
        u_out[0] = u[tm - 8:]
    else:
        pos = row % seq_len
        um1 = jnp.where(pos >= 1, r1, i1_ref[0])
        um2 = jnp.where(pos >= 2, r2, i2_ref[0])
        u_out[0] = u
    cw = cw_ref[0]
    y = cw[0:1] * um2 + cw[1:2] * um1 + cw[2:3] * u
    zb = mm(C_ZB, C_GL)
    cp_out[0] = (mm(C_BG, C_CG) * y * (zb * _sigmoid(zb))).astype(BF16)


def _inproj(layer, x, w_r, gn, gq_t, gks_t, gkw_t, gmat, conv_w, seq_len, tm, inits=None):
    nb, t = x.shape[:2]
    carried = seq_len >= tm
    grid = (nb, t // tm)
    tok = lambda n: pl.BlockSpec((1, tm, n), lambda b, s: (b, s, 0))
    lay2 = lambda n: pl.BlockSpec((1, 1, n), lambda b, s: (layer, 0, 0))
    in_specs = [
        tok(D_MODEL),
        lay2(D_MODEL),
        pl.BlockSpec((1, D_MODEL, C_END), lambda b, s: (layer, 0, 0)),
        lay2(512), lay2(128), lay2(128),
        pl.BlockSpec((256, 256), lambda b, s: (0, 0)),
        pl.BlockSpec((1, CONV_W, D_CONV), lambda b, s: (layer, 0, 0)),
    ]
    args = [x, gn, w_r, gq_t, gks_t, gkw_t, gmat, conv_w]
    scratch = []
    if carried:
        u_spec = pl.BlockSpec((1, 8, D_CONV), lambda b, s: (b, 0, 0))
        u_shape = (nb, 8, D_CONV)
        scratch = [pltpu.VMEM((8, D_CONV), F32)]
    else:
        in_specs += [tok(D_CONV), tok(D_CONV)]
        args += list(inits)
        u_spec = tok(D_CONV)
        u_shape = (nb, t, D_CONV)
    out_shape = [
        jax.ShapeDtypeStruct((nb, t, 512), BF16),
        jax.ShapeDtypeStruct((nb, t, 512), F32),
        jax.ShapeDtypeStruct((nb, t, 256), F32),
        jax.ShapeDtypeStruct((nb, t, 128), F32),
        jax.ShapeDtypeStruct((nb, t, 512), F32),
        jax.ShapeDtypeStruct((nb, t, 512), BF16),
        jax.ShapeDtypeStruct(u_shape, F32),
    ]
    out_specs = [tok(512), tok(512), tok(256), tok(128), tok(512), tok(512), u_spec]
    return pl.pallas_call(
        functools.partial(_inproj_kernel, seq_len, tm),
        grid=grid, in_specs=in_specs, out_specs=out_specs, out_shape=out_shape,
        scratch_shapes=scratch,
        compiler_params=pltpu.CompilerParams(
            dimension_semantics=("arbitrary", "arbitrary"), vmem_limit_bytes=VMEM_LIMIT),
        name="in_proj",
    )(*args)


def _compress_rows(load, n_half, wc_ref, pos_ref, gk_ref, gm_ref, out_ref):
    acc = jnp.zeros((2 * n_half, 256), F32)
    for l in range(CMP_BLOCK):
        xl = jnp.concatenate(
            [jnp.concatenate([load(half, l, n_half, 2 * CMP_BLOCK),
                              load(half, CMP_BLOCK + l, n_half, 2 * CMP_BLOCK)], axis=0)
             for half in range(2)], axis=1)
        xl = (xl + pos_ref[0, l:l + 1, :]).astype(BF16)
        acc = acc + jnp.dot(xl, wc_ref[0, l], preferred_element_type=F32)
    out_ref[0, :, 0:128] = _group_norm(acc[:, 0:128], gk_ref[0], gm_ref[...])
    out_ref[0, :, 128:256] = acc[:, 128:256]


def _compress_prompt_kernel(n_half, kc_ref, vc_ref, wc_ref, pos_ref, gk_ref, gm_ref, out_ref):
    refs = (kc_ref, vc_ref)
    load = lambda half, start, n, stride: refs[half][0, pl.ds(start, n, stride=stride), :]
    _compress_rows(load, n_half, wc_ref, pos_ref, gk_ref, gm_ref, out_ref)


def _compress_prompt(layer, kv, wc, posc, gk0_t, gmat):
    nb, s = kv.shape[:2]
    nc = s // CMP_BLOCK
    return pl.pallas_call(
        functools.partial(_compress_prompt_kernel, nc // 2),
        grid=(nb,),
        in_specs=[
            pl.BlockSpec((1, s, 128), lambda b: (b, 0, 0)),
            pl.BlockSpec((1, s, 128), lambda b: (b, 0, 1)),
            pl.BlockSpec((1, CMP_BLOCK, 256, 256), lambda b: (layer, 0, 0, 0)),
            pl.BlockSpec((1, CMP_BLOCK, 256), lambda b: (layer, 0, 0)),
            pl.BlockSpec((1, 1, 128), lambda b: (layer, 0, 0)),
            pl.BlockSpec((256, 256), lambda b: (0, 0)),
        ],
        out_specs=pl.BlockSpec((1, nc, 256), lambda b: (b, 0, 0)),
        out_shape=jax.ShapeDtypeStruct((nb, nc, 256), F32),
        compiler_params=pltpu.CompilerParams(
            dimension_semantics=("arbitrary",), vmem_limit_bytes=VMEM_LIMIT),
        name="compress_prompt",
    )(kv, kv, wc, posc, gk0_t, gmat)


def _topk_rows(score, blk, n):
    sel = jnp.zeros(score.shape, F32)
    for _ in range(n):
        mx = jnp.max(score, axis=0, keepdims=True)
        idx = jnp.min(jnp.where(score == mx, blk, 1 << 20), axis=0, keepdims=True)
        hit = blk == idx
        sel = jnp.where(hit, 1.0, sel)
        score = jnp.where(hit, -3e38, score)
    return sel


def _expand_gates(g, e_ref):
    g_hi = g.astype(BF16)
    g_lo = (g - g_hi.astype(F32)).astype(BF16)
    e = e_ref[...]
    return (jnp.dot(g_hi, e, preferred_element_type=F32)
            + jnp.dot(g_lo, e, preferred_element_type=F32))


V_EXT = HEAD_DIM + 16


def _masked_exp(st, valid):
    st = jnp.where(valid, st, NEG)
    mx = jnp.max(st, axis=0, keepdims=True)
    return jnp.exp(st - mx), mx


def _normalize_ext(o_ext):
    return o_ext[0:HEAD_DIM] * (1.0 / o_ext[HEAD_DIM:HEAD_DIM + 1])


def _attn_prompt_t_kernel(s_len, q_ref, kv_ref, win_ref, cmp_ref, gate_ref, sa_ref,
                          pc_ref, mk_ref, cc_ref, o_ref,
                          ksa, vst, kwa, vwt, kca, vct, qa, acc_scr, m_scr, s_a, s_b):
    qi = pl.program_id(1)
    q0 = qi * Q_TILE
    n_cmp = s_len // CMP_BLOCK
    n_blk = s_len // SEL_BLOCK
    w_keys = WINDOW + Q_TILE

    @pl.when(qi == 0)
    def _build():
        def bld(c, carry):
            r = pl.ds(pl.multiple_of(c * 512, 512), 512)
            kvb = kv_ref[0, r, :]
            wb = win_ref[0, r, :]
            pc = pc_ref[r, :]
            lane = _lane_iota(pc.shape)
            for k in range(N_KV):
                ks = kvb[:, 0:128] if k == 0 else pltpu.roll(kvb[:, 0:128], 64, axis=1)
                kw = wb[:, 0:128] if k == 0 else pltpu.roll(wb[:, 0:128], 64, axis=1)
                ksa[k, r, 0:128] = jnp.where(lane < 64, ks, pc).astype(BF16)
                ksa[k, r, 128:256] = mk_ref[r, :]
                kwa[k, r, :] = jnp.where(lane < 64, kw, pc).astype(BF16)
            ones = jnp.ones((V_EXT - HEAD_DIM, LANES), BF16)
            for j in range(512 // LANES):
                rows = slice(j * LANES, (j + 1) * LANES)
                for dst, src in ((vst, kvb), (vwt, wb)):
                    t = src[rows, 128:256].T.astype(BF16)
                    tile = jnp.concatenate([t[0:HEAD_DIM], ones, t[HEAD_DIM:], ones], axis=0)
                    dst[c * (512 // LANES) + j] = tile
            return carry
        lax.fori_loop(0, s_len // 512, bld, 0)
        cm = cmp_ref[0]
        cc = cc_ref[...]
        lane = _lane_iota(cc.shape)
        for k in range(N_KV):
            kc = cm[:, 0:128] if k == 0 else pltpu.roll(cm[:, 0:128], 64, axis=1)
            kca[k] = jnp.where(lane < 64, kc, cc).astype(BF16)
        vct[...] = cm[:, 128:256].T.astype(BF16)

    qf = q_ref[0].astype(F32)
    lane128 = _lane_iota((Q_TILE, LANES))
    tq = q0 + (_lane_iota((1, GROUP * Q_TILE)) % Q_TILE)

    for k in range(N_KV):
        for h in range(GROUP):
            hg = k * GROUP + h
            tile = qf[:, (hg // 2) * 128:(hg // 2 + 1) * 128]
            src = tile if hg % 2 == 0 else pltpu.roll(tile, 64, axis=1)
            m = _slope(hg)
            bias = jnp.where(lane128 == 64, 64.0 * m,
                             jnp.where(lane128 == 65, m,
                                       jnp.where(lane128 == 66, -m * q0.astype(F32), 0.0)))
            qa[k, h * Q_TILE:(h + 1) * Q_TILE, 0:128] = jnp.where(lane128 < 64, src, bias).astype(BF16)

    oct_ = []
    for k in range(N_KV):
        dims = slice(k * HEAD_DIM, (k + 1) * HEAD_DIM)
        st = lax.dot_general(kca[k], qa[k, :, 0:128], _NT, preferred_element_type=F32)
        r = _row_iota(st.shape)
        cend = CMP_BLOCK * (2 * (r % (n_cmp // 2)) + r // (n_cmp // 2)) + (CMP_BLOCK - 1)
        e, mx = _masked_exp(st, cend <= tq)
        den = jnp.maximum(jnp.sum(e, axis=0, keepdims=True), 1e-30)
        pt = e * jnp.where(mx > 0.5 * NEG, 1.0 / den, 0.0)
        oct_.append(jnp.dot(vct[dims, :], pt.astype(BF16), preferred_element_type=F32))

        imp = pt[:, 0:128] + pt[:, 128:256] + pt[:, 256:384] + pt[:, 384:512]
        imp = imp[0:n_blk] + imp[n_blk:2 * n_blk]
        blk = _row_iota(imp.shape)
        tcol = q0 + _lane_iota(imp.shape)
        cur = tcol // SEL_BLOCK
        forced = (blk == 0) | (blk == cur) | (blk == cur - 1)
        score = jnp.where(blk * SEL_BLOCK <= tcol,
                          imp + jnp.where(forced, FORCE_BONUS, 0.0), INVALID)
        sel = _topk_rows(score, blk, min(TOP_N, n_blk))
        notsel = jnp.concatenate([1.0 - sel, jnp.zeros((LANES - n_blk, Q_TILE), F32)], axis=0)
        notsel = notsel.T.astype(BF16)
        for h in range(GROUP):
            qa[k, h * Q_TILE:(h + 1) * Q_TILE, 128:256] = notsel

    SEL, WIN = 0, 1

    def scores_into(dst, branch, off, nkeys):
        for k in range(N_KV):
            if branch == SEL:
                s = lax.dot_general(ksa[k, pl.ds(off, nkeys), :], qa[k], _NT,
                                    preferred_element_type=F32)
            else:
                s = lax.dot_general(kwa[k, pl.ds(off, nkeys), :], qa[k, :, 0:128], _NT,
                                    preferred_element_type=F32)
            dst[k, 0:nkeys] = s

    def consume(src, branch, off, nkeys, masked):
        vals = vst if branch == SEL else vwt
        for k in range(N_KV):
            s = src[k, 0:nkeys]
            if masked:
                dist = tq - (off + _row_iota(s.shape))
                ok = (dist >= 0) if branch == SEL else (dist.astype(jnp.uint32) < WINDOW)
                s = jnp.where(ok, s, NEG)
            m_old = m_scr[branch, k]
            m_new = jnp.maximum(m_old, jnp.max(s, axis=0, keepdims=True))
            alpha = jnp.exp(m_old - m_new)
            p = jnp.exp(s - m_new)
            v = vals[pl.ds(off // LANES, nkeys // LANES), k * V_EXT:(k + 1) * V_EXT, :]
            v = jnp.concatenate([v[j] for j in range(nkeys // LANES)], axis=1)
            acc_scr[branch, k] = alpha * acc_scr[branch, k] + jnp.dot(
                v, p.astype(BF16), preferred_element_type=F32)
            m_scr[branch, k] = m_new

    acc_scr[...] = jnp.zeros(acc_scr.shape, F32)
    m_scr[...] = jnp.full(m_scr.shape, -3e38, F32)

    sel_off = lambda c: pl.multiple_of(c * SEL_CHUNK, SEL_CHUNK)
    w0 = pl.multiple_of(jnp.maximum(q0 - WINDOW, 0), Q_TILE)
    w1 = pl.multiple_of(w0 + SEL_CHUNK, Q_TILE)
    w2 = pl.multiple_of(w0 + 2 * SEL_CHUNK, Q_TILE)
    w_last = w_keys - 2 * SEL_CHUNK
    n_full = q0 // SEL_CHUNK
    n_pairs = n_full // 2

    scores_into(s_a, WIN, w0, SEL_CHUNK)
    scores_into(s_b, WIN, w1, SEL_CHUNK)
    consume(s_a, WIN, w0, SEL_CHUNK, True)
    scores_into(s_a, WIN, w2, w_last)
    consume(s_b, WIN, w1, SEL_CHUNK, True)
    scores_into(s_b, SEL, sel_off(0), SEL_CHUNK)
    consume(s_a, WIN, w2, w_last, True)

    def body(i, carry):
        scores_into(s_a, SEL, sel_off(2 * i + 1), SEL_CHUNK)
        consume(s_b, SEL, sel_off(2 * i), SEL_CHUNK, False)
        scores_into(s_b, SEL, sel_off(2 * i + 2), SEL_CHUNK)
        consume(s_a, SEL, sel_off(2 * i + 1), SEL_CHUNK, False)
        return carry
    lax.fori_loop(0, n_pairs, body, 0)

    @pl.when(n_full % 2 == 0)
    def _():
        consume(s_b, SEL, sel_off(n_full), SEL_CHUNK, True)

    @pl.when(n_full % 2 == 1)
    def _():
        scores_into(s_a, SEL, sel_off(n_full), SEL_CHUNK)
        consume(s_b, SEL, sel_off(n_full - 1), SEL_CHUNK, False)
        consume(s_a, SEL, sel_off(n_full), SEL_CHUNK, True)

    gt = gate_ref[0].T
    sa = sa_ref[0]
    for k in range(N_KV):
        ost = _normalize_ext(acc_scr[SEL, k])
        owk = _normalize_ext(acc_scr[WIN, k])
        for pair in range(GROUP // 2):
            halves = []
            for h in (2 * pair, 2 * pair + 1):
                hg = k * GROUP + h
                cols = slice(h * Q_TILE, (h + 1) * Q_TILE)
                halves.append(gt[3 * hg:3 * hg + 1] * oct_[k][:, cols]
                              + gt[3 * hg + 1:3 * hg + 2] * ost[:, cols]
                              + gt[3 * hg + 2:3 * hg + 3] * owk[:, cols])
            t = k * 2 + pair
            o = jnp.concatenate(halves, axis=0).T
            o_ref[0, :, t * 128:(t + 1) * 128] = (o * sa[:, t * 128:(t + 1) * 128]).astype(BF16)


def _attn_prompt_t(q, kv, win, cmpk, gates, sa, pc, mk, cc):
    nb, s = q.shape[:2]
    nq = s // Q_TILE
    n_cmp = s // CMP_BLOCK
    tokq = lambda n: pl.BlockSpec((1, Q_TILE, n), lambda b, i: (b, i, 0))
    const = lambda a: pl.BlockSpec(a.shape, lambda b, i: (0,) * a.ndim)
    return pl.pallas_call(
        functools.partial(_attn_prompt_t_kernel, s),
        grid=(nb, nq),
        in_specs=[
            tokq(512),
            pl.BlockSpec((1, s, 256), lambda b, i: (b, 0, 1)),
            pl.BlockSpec((1, s, 256), lambda b, i: (b, 0, 0)),
            pl.BlockSpec((1, n_cmp, 256), lambda b, i: (b, 0, 0)),
            tokq(128), tokq(512),
            const(pc), const(mk), const(cc),
        ],
        out_specs=tokq(512),
        out_shape=jax.ShapeDtypeStruct((nb, s, 512), BF16),
        scratch_shapes=[
            pltpu.VMEM((N_KV, s, 256), BF16),
            pltpu.VMEM((s // LANES, N_KV * V_EXT, LANES), BF16),
            pltpu.VMEM((N_KV, s, 128), BF16),
            pltpu.VMEM((s // LANES, N_KV * V_EXT, LANES), BF16),
            pltpu.VMEM((N_KV, n_cmp, 128), BF16),
            pltpu.VMEM((128, n_cmp), BF16),
            pltpu.VMEM((N_KV, GROUP * Q_TILE, 256), BF16),
            pltpu.VMEM((2, N_KV, V_EXT, GROUP * Q_TILE), F32),
            pltpu.VMEM((2, N_KV, 1, GROUP * Q_TILE), F32),
            pltpu.VMEM((N_KV, SEL_CHUNK, GROUP * Q_TILE), F32),
            pltpu.VMEM((N_KV, SEL_CHUNK, GROUP * Q_TILE), F32),
        ],
        compiler_params=pltpu.CompilerParams(
            dimension_semantics=("arbitrary", "arbitrary"), vmem_limit_bytes=VMEM_LIMIT),
        name="attn_prompt",
    )(q, kv, win, cmpk, gates, sa, pc, mk, cc)


def _outproj_kernel(x_ref, oa_ref, cp_ref, p_ref, wo_ref, wple_ref, wpg_ref, g_ref, y_ref):
    x1 = (x_ref[0]
          + jnp.dot(oa_ref[0], wo_ref[0, 0:D_ATT, :], preferred_element_type=F32)
          + jnp.dot(cp_ref[0], wo_ref[0, D_ATT:, :], preferred_element_type=F32))
    ms = jnp.mean(x1 * x1, axis=-1, keepdims=True)
    h2 = (x1 * lax.rsqrt(ms + EPS) * g_ref[0]).astype(BF16)
    gate = _sigmoid(jnp.dot(h2, wpg_ref[0], preferred_element_type=F32))
    pe = jnp.dot(p_ref[0, 0].astype(BF16), wple_ref[0], preferred_element_type=F32)
    y_ref[0] = x1 + pe * gate


def _outproj(layer, x, oa, cp, p_all, wo, wple, wpg, g_ple, tm):
    nb, t = x.shape[:2]
    tok = lambda n: pl.BlockSpec((1, tm, n), lambda b, s: (b, s, 0))
    return pl.pallas_call(
        _outproj_kernel,
        grid=(nb, t // tm),
        in_specs=[
            tok(D_MODEL), tok(512), tok(512),
            pl.BlockSpec((1, 1, tm, D_PLE), lambda b, s: (layer, b, s, 0)),
            pl.BlockSpec((1, D_MODEL, D_MODEL), lambda b, s: (layer, 0, 0)),
            pl.BlockSpec((1, D_PLE, D_MODEL), lambda b, s: (layer, 0, 0)),
            pl.BlockSpec((1, D_MODEL, D_MODEL), lambda b, s: (layer, 0, 0)),
            pl.BlockSpec((1, 1, D_MODEL), lambda b, s: (layer, 0, 0)),
        ],
        out_specs=tok(D_MODEL),
        out_shape=jax.ShapeDtypeStruct((nb, t, D_MODEL), F32),
        compiler_params=pltpu.CompilerParams(
            dimension_semantics=("arbitrary", "arbitrary"), vmem_limit_bytes=VMEM_LIMIT),
        name="out_proj",
    )(x, oa, cp, p_all, wo, wple, wpg, g_ple)


def _page_copy(cache_ref, pt_ref, buf, sem, b, slot, n_pages, row0):
    return [pltpu.make_async_copy(
        cache_ref.at[pt_ref[b, p], pl.ds(row0, 256), :],
        buf.at[slot, pl.ds(p * 256, 256), :],
        sem.at[slot]) for p in range(n_pages)]


def _paged_prefetch(cache_ref, pt_ref, buf, sem, n_pages, row0):
    b = pl.program_id(0)
    slot = b % 2

    @pl.when(b == 0)
    def _():
        for cp in _page_copy(cache_ref, pt_ref, buf, sem, 0, 0, n_pages, row0):
            cp.start()

    @pl.when(b + 1 < pl.num_programs(0))
    def _():
        for cp in _page_copy(cache_ref, pt_ref, buf, sem, b + 1, 1 - slot, n_pages, row0):
            cp.start()

    for cp in _page_copy(cache_ref, pt_ref, buf, sem, b, slot, n_pages, row0):
        cp.wait()
    return slot


def _compress_sample_kernel(n_pages, pt_ref, cache_ref, wc_ref, post_ref, perm_ref, gk_ref, gm_ref,
                            out_ref, buf, sem, zbuf):
    slot = _paged_prefetch(cache_ref, pt_ref, buf, sem, n_pages, 0)
    per_pair = 2 * PAGE_SIZE // CMP_BLOCK

    def regroup(g, carry):
        xt = jnp.concatenate(
            [buf[slot, pl.ds(pl.multiple_of((2 * g + j) * 256, 256), 256), :] for j in range(2)],
            axis=1)
        xt = (xt + post_ref[0]).astype(BF16)
        z = lax.dot_general(perm_ref[...], xt, _NT, preferred_element_type=F32)
        rows = pl.ds(pl.multiple_of(g * per_pair, per_pair), per_pair)
        for l in range(CMP_BLOCK):
            zbuf[l, rows, :] = z[l * per_pair:(l + 1) * per_pair]

    def regroup4(g4, carry):
        for j in range(4):
            regroup(4 * g4 + j, carry)
        return carry
    lax.fori_loop(0, n_pages // 8, regroup4, 0)

    acc = jnp.zeros((zbuf.shape[1], 256), F32)
    for l in range(CMP_BLOCK):
        acc = acc + jnp.dot(zbuf[l].astype(BF16), wc_ref[0, l], preferred_element_type=F32)
    out_ref[0, :, 0:128] = _group_norm(acc[:, 0:128], gk_ref[0], gm_ref[...])
    out_ref[0, :, 128:256] = acc[:, 128:256]


def _compress_sample(layer, pt, cache3, wc, post, perm, gk0_t, gmat):
    nb, n_pages = pt.shape
    nc = n_pages * PAGE_SIZE // CMP_BLOCK
    grid_spec = pltpu.PrefetchScalarGridSpec(
        num_scalar_prefetch=1,
        grid=(nb,),
        in_specs=[
            pl.BlockSpec(memory_space=pl.ANY),
            pl.BlockSpec((1, CMP_BLOCK, 256, 256), lambda b, pt: (layer, 0, 0, 0)),
            pl.BlockSpec((1, 256, 2 * PAGE_SIZE), lambda b, pt: (layer, 0, 0)),
            pl.BlockSpec((2 * PAGE_SIZE, 2 * PAGE_SIZE), lambda b, pt: (0, 0)),
            pl.BlockSpec((1, 1, 128), lambda b, pt: (layer, 0, 0)),
            pl.BlockSpec((256, 256), lambda b, pt: (0, 0)),
        ],
        out_specs=pl.BlockSpec((1, nc, 256), lambda b, pt: (b, 0, 0)),
        scratch_shapes=[
            pltpu.VMEM((2, n_pages * 256, PAGE_SIZE), F32),
            pltpu.SemaphoreType.DMA((2,)),
            pltpu.VMEM((CMP_BLOCK, nc, 256), F32),
        ],
    )
    return pl.pallas_call(
        functools.partial(_compress_sample_kernel, n_pages),
        grid_spec=grid_spec,
        out_shape=jax.ShapeDtypeStruct((nb, nc, 256), F32),
        compiler_params=pltpu.CompilerParams(
            dimension_semantics=("arbitrary",), vmem_limit_bytes=VMEM_LIMIT),
        name="compress_sample",
    )(pt, cache3, wc, post, perm, gk0_t, gmat)


def _attn_sample_kernel(n_pages, dec_len, pt_ref, cache_ref, q_ref, kvn_ref, wn_ref,
                        cwin_ref, cmp_ref, gate_ref, sa_ref, mk_ref, e_ref, pair_ref, o_ref,
                        buf, sem, kta, vtb):
    past = n_pages * PAGE_SIZE
    n_cmp = past // CMP_BLOCK
    n_blk = past // SEL_BLOCK
    rows = N_KV * GROUP * dec_len
    hrows = GROUP * dec_len

    @pl.when(pl.program_id(0) == 0)
    def _():
        kta[128:256, :] = mk_ref[...]

    slot = _paged_prefetch(cache_ref, pt_ref, buf, sem, n_pages, 256)
    for p in range(n_pages):
        blk = buf[slot, p * 256:(p + 1) * 256, :]
        kta[0:128, p * PAGE_SIZE:(p + 1) * PAGE_SIZE] = blk[0:128].astype(BF16)
        vtb[:, p * PAGE_SIZE:(p + 1) * PAGE_SIZE] = blk[128:256].astype(BF16)

    qf = q_ref[0].astype(F32)
    lane = _lane_iota((dec_len, LANES))
    pieces = []
    for k in range(N_KV):
        for h in range(GROUP):
            hg = k * GROUP + h
            tile = qf[:, (hg // 2) * 128:(hg // 2 + 1) * 128]
            src = tile if (hg % 2) == k else pltpu.roll(tile, 64, axis=1)
            keep = (lane < 64) if k == 0 else (lane >= 64)
            pieces.append(jnp.where(keep, src, 0.0))
    qbd = jnp.concatenate(pieces, axis=0)
    qbd16 = qbd.astype(BF16)

    rid = _row_iota((rows, 1))
    slope = jnp.zeros((rows, 1), F32)
    for hg in range(N_HEADS):
        slope = jnp.where(rid // dec_len == hg, _slope(hg), slope)
    qpos = rid % dec_len

    cm = cmp_ref[0]
    sc = lax.dot_general(qbd16, cm[:, 0:128].astype(BF16), _NT, preferred_element_type=F32)
    cend = CMP_BLOCK * _lane_iota(sc.shape) + (CMP_BLOCK - 1)
    sc = sc + slope * (cend - past).astype(F32)
    mc = jnp.max(sc, axis=1, keepdims=True)
    ec = jnp.exp(sc - mc)
    pcm = ec / jnp.maximum(jnp.sum(ec, axis=1, keepdims=True), 1e-30)
    ocm = jnp.dot(pcm.astype(BF16), cm[:, 128:256].astype(BF16), preferred_element_type=F32)

    scores = []
    for k in range(N_KV):
        imp = pcm[k * hrows:k * hrows + dec_len]
        for h in range(1, GROUP):
            imp = imp + pcm[k * hrows + h * dec_len:k * hrows + (h + 1) * dec_len]
        hi = imp.astype(BF16)
        mid = (imp - hi.astype(F32)).astype(BF16)
        lo = (imp - hi.astype(F32) - mid.astype(F32)).astype(BF16)
        imp = sum(jnp.dot(part, pair_ref[...], preferred_element_type=F32)
                  for part in (hi, mid, lo))
        blk = _lane_iota(imp.shape)
        t = past + _row_iota(imp.shape)
        cur = t // SEL_BLOCK
        forced = (blk == 0) | (blk == cur) | (blk == cur - 1)
        score = imp + jnp.where(forced, FORCE_BONUS, 0.0)
        if n_blk < LANES:
            score = jnp.concatenate([score, jnp.full((dec_len, LANES - n_blk), -3e38, F32)], axis=1)
        scores.append(score)
    score_t = jnp.concatenate(
        scores + [jnp.zeros((LANES - N_KV * dec_len, LANES), F32)], axis=0).T
    sel_t = _topk_rows(score_t, _row_iota(score_t.shape), TOP_N - 1)
    ns = 1.0 - sel_t.T
    notsel = jnp.concatenate(
        [ns[k * dec_len:(k + 1) * dec_len] for k in range(N_KV) for _ in range(GROUP)], axis=0)
    qaug = jnp.concatenate([qbd16, notsel.astype(BF16)], axis=1)

    s1 = jnp.dot(qaug, kta[...], preferred_element_type=F32)
    s1 = s1 + slope * (_lane_iota(s1.shape) - past).astype(F32)
    kn = jnp.concatenate([kvn_ref[0][:, 256:384], jnp.zeros((LANES - dec_len, 128), F32)], axis=0)
    vn = jnp.concatenate([kvn_ref[0][:, 384:512], jnp.zeros((LANES - dec_len, 128), F32)], axis=0)
    s2 = lax.dot_general(qbd16, kn.astype(BF16), _NT, preferred_element_type=F32)
    j2 = _lane_iota(s2.shape)
    s2 = jnp.where(j2 <= qpos, s2 + slope * j2.astype(F32), NEG)
    ms = jnp.maximum(jnp.max(s1, axis=1, keepdims=True), jnp.max(s2, axis=1, keepdims=True))
    p1 = jnp.exp(s1 - ms)
    p2 = jnp.exp(s2 - ms)
    ls = jnp.sum(p1, axis=1, keepdims=True) + jnp.sum(p2, axis=1, keepdims=True)
    osl = (lax.dot_general(p1.astype(BF16), vtb[...], _NT, preferred_element_type=F32)
           + jnp.dot(p2.astype(BF16), vn.astype(BF16), preferred_element_type=F32)) / ls

    cw = cwin_ref[0, 0]
    wb = cw.shape[1]
    w1 = jnp.dot(qbd16, cw[0:128].astype(BF16), preferred_element_type=F32)
    jw = _lane_iota(w1.shape)
    dist = qpos + (wb - jw)
    w1 = jnp.where(dist < WINDOW, w1 + slope * (jw - wb).astype(F32), NEG)
    kwn = jnp.concatenate([wn_ref[0][:, 0:128], jnp.zeros((LANES - dec_len, 128), F32)], axis=0)
    vwn = jnp.concatenate([wn_ref[0][:, 128:256], jnp.zeros((LANES - dec_len, 128), F32)], axis=0)
    w2 = lax.dot_general(qbd16, kwn.astype(BF16), _NT, preferred_element_type=F32)
    w2 = jnp.where(j2 <= qpos, w2 + slope * j2.astype(F32), NEG)
    mw = jnp.maximum(jnp.max(w1, axis=1, keepdims=True), jnp.max(w2, axis=1, keepdims=True))
    pw1 = jnp.exp(w1 - mw)
    pw2 = jnp.exp(w2 - mw)
    lw = jnp.sum(pw1, axis=1, keepdims=True) + jnp.sum(pw2, axis=1, keepdims=True)
    owd = (lax.dot_general(pw1.astype(BF16), cw[128:256].astype(BF16), _NT,
                           preferred_element_type=F32)
           + jnp.dot(pw2.astype(BF16), vwn.astype(BF16), preferred_element_type=F32)) / lw

    ge = _expand_gates(gate_ref[0], e_ref)
    sa = sa_ref[0]
    for t4 in range(4):
        k, pair = t4 // 2, t4 % 2
        r0 = (k * GROUP + 2 * pair) * dec_len
        r1 = r0 + dec_len
        rows_of = lambda a, r: a[r:r + dec_len]
        sl = slice(t4 * 128, (t4 + 1) * 128)
        o = (ge[:, sl] * _place_pair(rows_of(ocm, r0), rows_of(ocm, r1), k)
             + ge[:, 512 + t4 * 128:512 + (t4 + 1) * 128]
             * _place_pair(rows_of(osl, r0), rows_of(osl, r1), k)
             + ge[:, 1024 + t4 * 128:1024 + (t4 + 1) * 128]
             * _place_pair(rows_of(owd, r0), rows_of(owd, r1), k))
        o_ref[0, :, sl] = (o * sa[:, sl]).astype(BF16)


def _attn_sample(layer, pt, cache3, q, kvn, wn, cwin, cmpk, gates, sa, mk, emat, pairm):
    nb, n_pages = pt.shape
    dec_len = q.shape[1]
    past = n_pages * PAGE_SIZE
    wb = cwin.shape[3]
    tok = lambda n: pl.BlockSpec((1, dec_len, n), lambda b, pt: (b, 0, 0))
    grid_spec = pltpu.PrefetchScalarGridSpec(
        num_scalar_prefetch=1,
        grid=(nb,),
        in_specs=[
            pl.BlockSpec(memory_space=pl.ANY),
            tok(512), tok(512), tok(256),
            pl.BlockSpec((1, 1, 256, wb), lambda b, pt: (layer, b, 0, 0)),
            pl.BlockSpec((1, past // CMP_BLOCK, 256), lambda b, pt: (b, 0, 0)),
            tok(128), tok(512),
            pl.BlockSpec(mk.shape, lambda b, pt: (0, 0)),
            pl.BlockSpec(emat.shape, lambda b, pt: (0, 0)),
            pl.BlockSpec(pairm.shape, lambda b, pt: (0, 0)),
        ],
        out_specs=tok(512),
        scratch_shapes=[
            pltpu.VMEM((2, n_pages * 256, PAGE_SIZE), F32),
            pltpu.SemaphoreType.DMA((2,)),
            pltpu.VMEM((256, past), BF16),
            pltpu.VMEM((128, past), BF16),
        ],
    )
    return pl.pallas_call(
        functools.partial(_attn_sample_kernel, n_pages, dec_len),
        grid_spec=grid_spec,
        out_shape=jax.ShapeDtypeStruct((nb, dec_len, 512), BF16),
        compiler_params=pltpu.CompilerParams(
            dimension_semantics=("arbitrary",), vmem_limit_bytes=VMEM_LIMIT),
        name="attn_sample",
    )(pt, cache3, q, kvn, wn, cwin, cmpk, gates, sa, mk, emat, pairm)


def _key_consts(n_keys, n_cmp):
    kpos = np.arange(n_keys)
    pc = np.zeros((n_keys, LANES), np.float32)
    pc[:, 64] = kpos // 64
    pc[:, 65] = kpos % 64
    pc[:, 66] = 1.0
    mk = np.zeros((n_keys, LANES), np.float32)
    mk[kpos, (kpos // SEL_BLOCK) % LANES] = -BIG
    r = np.arange(n_cmp)
    cend = CMP_BLOCK * (2 * (r % (n_cmp // 2)) + r // (n_cmp // 2)) + CMP_BLOCK - 1
    cc = np.zeros((n_cmp, LANES), np.float32)
    cc[:, 64] = cend // 64
    cc[:, 65] = cend % 64
    cc[:, 66] = 1.0
    return jnp.asarray(pc), jnp.asarray(mk, dtype=BF16), jnp.asarray(cc)


def _block_mask_rows(n_keys):
    kpos = np.arange(n_keys)
    mk = np.zeros((LANES, n_keys), np.float32)
    mk[(kpos // SEL_BLOCK) % LANES, kpos] = -BIG
    return jnp.asarray(mk, dtype=BF16)


def _gate_expand_matrix():
    e = np.zeros((LANES, 3 * D_ATT), np.float32)
    for h in range(N_HEADS):
        for j in range(3):
            e[h * 3 + j, j * D_ATT + h * HEAD_DIM:j * D_ATT + (h + 1) * HEAD_DIM] = 1.0
    return jnp.asarray(e, dtype=BF16)


def _group_mean_matrix():
    i = np.arange(256)
    return jnp.asarray((i[:, None] // HEAD_DIM == i[None, :] // HEAD_DIM) / HEAD_DIM, dtype=BF16)


def _reorder_w_in(w_in):
    o = np.cumsum([0, 512, 128, 128, 128, 128, 128, 128, 24, 512, 512, 512, 512, 512])
    q, kc, vc, ks, vs, kw, vw, gl, za, bg, cg, hv, zb = [
        w_in[..., o[i]:o[i + 1]] for i in range(13)]
    pad = jnp.zeros(w_in.shape[:-1] + (LANES - 24,), w_in.dtype)
    return jnp.concatenate([q, kc, vc, ks, vs, kw, vw, za, bg, cg, hv, zb, gl, pad],
                           axis=-1).astype(BF16)


def _compress_weights(w_phi, cmp_pos):
    w16 = w_phi.astype(BF16)
    zero = jnp.zeros_like(w16[:, 0])
    blocks = [w16[:, 0], w16[:, 0], w16[:, 1], w16[:, 1]]
    wc = jnp.concatenate(
        [jnp.concatenate([blocks[i] if i == j else zero for j in range(4)], axis=-1)
         for i in range(4)], axis=-2)
    posc = jnp.concatenate([cmp_pos[:, 0], cmp_pos[:, 0], cmp_pos[:, 1], cmp_pos[:, 1]], axis=-1)
    post = jnp.tile(jnp.transpose(posc, (0, 2, 1)), (1, 1, 2 * PAGE_SIZE // CMP_BLOCK))
    return wc, posc, post


def _regroup_matrix():
    n = 2 * PAGE_SIZE
    per_pair = n // CMP_BLOCK
    p = np.zeros((n, n), np.float32)
    for l in range(CMP_BLOCK):
        for c in range(per_pair):
            p[l * per_pair + c, CMP_BLOCK * c + l] = 1.0
    return jnp.asarray(p, dtype=BF16)


def _pair_sum_matrix(n_cmp):
    p = np.zeros((n_cmp, n_cmp // 2), np.float32)
    p[np.arange(n_cmp), np.arange(n_cmp) // 2] = 1.0
    return jnp.asarray(p, dtype=BF16)


def kernel(x_prompt, x_sample, cache_kv, cache_win, state_conv, page_table, p_prompt, p_sample,
           g_norm, w_in, g_q, g_k, cmp_pos, w_phi, conv_w, w_out, w_ple, w_pg, g_ple):
    depth = w_in.shape[0]
    nb, s_len = x_prompt.shape[:2]
    db, dec_len = x_sample.shape[:2]
    n_pool = cache_kv.shape[1]
    n_pages = page_table.shape[1]
    past = n_pages * PAGE_SIZE
    wb = cache_win.shape[2]

    w_r = _reorder_w_in(w_in)
    wo16, wple16, wpg16 = w_out.astype(BF16), w_ple.astype(BF16), w_pg.astype(BF16)
    wc, posc, post = _compress_weights(w_phi, cmp_pos)
    perm = _regroup_matrix()
    pairm = _pair_sum_matrix(past // CMP_BLOCK)
    gq_t = jnp.tile(g_q, (1, N_HEADS))[:, None]
    gk0_t = jnp.tile(g_k[:, 0], (1, N_KV))[:, None]
    gks_t = jnp.tile(g_k[:, 1], (1, N_KV))[:, None]
    gkw_t = jnp.tile(g_k[:, 2], (1, N_KV))[:, None]
    g_norm = g_norm[:, None]
    g_ple = g_ple[:, None]
    gmat = _group_mean_matrix()
    emat = _gate_expand_matrix()
    pc_p, mk_p, cc_p = _key_consts(s_len, s_len // CMP_BLOCK)
    mk_s = _block_mask_rows(past)

    cache3 = jnp.transpose(cache_kv, (0, 1, 3, 4, 5, 2)).reshape(
        depth * n_pool, 4 * N_KV * HEAD_DIM, PAGE_SIZE)
    cwin4 = jnp.transpose(cache_win, (0, 1, 3, 4, 5, 2)).reshape(
        depth, db, 2 * N_KV * HEAD_DIM, wb)
    xs = x_sample.reshape(1, db * dec_len, D_MODEL)
    ps = p_sample.reshape(depth, 1, db * dec_len, D_PLE)

    xp = x_prompt
    outs = [[] for _ in range(6)]
    for i in range(depth):
        q, kv, win, gates, sa, cp, tail = _inproj(
            i, xp, w_r, g_norm, gq_t, gks_t, gkw_t, gmat, conv_w, s_len, 512)
        cmpk = _compress_prompt(i, kv, wc, posc, gk0_t, gmat)
        oa = _attn_prompt_t(q, kv, win, cmpk, gates, sa, pc_p, mk_p, cc_p)
        xp = _outproj(i, xp, oa, cp, p_prompt, wo16, wple16, wpg16, g_ple, 512)
        outs[0].append(kv.reshape(nb, s_len, 4, N_KV, HEAD_DIM))
        outs[1].append(win[:, s_len - min(WINDOW, s_len):].reshape(nb, -1, 2, N_KV, HEAD_DIM))
        outs[2].append(tail[:, 8 - (CONV_W - 1):])

        st = state_conv[i]
        pad = jnp.zeros((db, dec_len - 1, D_CONV), F32)
        init1 = jnp.concatenate([st[:, 1:2], pad], axis=1).reshape(1, db * dec_len, D_CONV)
        init2 = jnp.concatenate([st[:, 0:1], st[:, 1:2], pad[:, 1:]], axis=1).reshape(
            1, db * dec_len, D_CONV)
        q, kv, win, gates, sa, cp, u = _inproj(
            i, xs, w_r, g_norm, gq_t, gks_t, gkw_t, gmat, conv_w, dec_len, db * dec_len,
            inits=(init1, init2))
        pt = page_table + i * n_pool
        cmpk = _compress_sample(i, pt, cache3, wc, post, perm, gk0_t, gmat)
        r3 = lambda a: a.reshape(db, dec_len, a.shape[-1])
        oa = _attn_sample(i, pt, cache3, r3(q), r3(kv), r3(win), cwin4, cmpk, r3(gates), r3(sa),
                          mk_s, emat, pairm)
        xs = _outproj(i, xs, oa.reshape(1, db * dec_len, 512), cp, ps, wo16, wple16, wpg16,
                      g_ple, db * dec_len)
        outs[3].append(kv.reshape(db, dec_len, 4, N_KV, HEAD_DIM))
        new_win = jnp.concatenate(
            [cwin4[i][:, :, dec_len:], jnp.transpose(r3(win), (0, 2, 1))], axis=2)
        outs[4].append(jnp.transpose(new_win.reshape(db, 2, N_KV, HEAD_DIM, wb), (0, 4, 1, 2, 3)))
        outs[5].append(u.reshape(db, dec_len, D_CONV)[:, dec_len - (CONV_W - 1):])

    return (xp, xs.reshape(db, dec_len, D_MODEL)) + tuple(jnp.stack(o) for o in outs)
```
